```python
import math
import jax, jax.numpy as jnp
from jax import lax
import numpy as np

D_MODEL = 1024
BATCH = 4
SEQ = 4096
DEPTH = 1
DEC_BATCH = 128
DEC_SEQ = 1
PAST_LEN = 8192
PAGE_SIZE = 128

SSD_EXPAND = 2
D_INNER = SSD_EXPAND * D_MODEL
SSD_HEAD_DIM = 64
SSD_HEADS = D_INNER // SSD_HEAD_DIM
SSD_GROUPS = 4
SSD_STATE = 128
CONV_W = 4
CONV_DIM = D_INNER + 2 * SSD_GROUPS * SSD_STATE
SSD_CHUNK = 128
DT_MIN = 0.001
DT_MAX = 0.1
N_HEADS = 16
N_KV_HEADS = 4
HEAD_DIM = 64
ATTN_WIDTH = N_HEADS * HEAD_DIM
KV_WIDTH = N_KV_HEADS * HEAD_DIM
IDX_HEADS = 8
IDX_DIM = 64
IDX_SCALE = (IDX_HEADS ** -0.5) * (IDX_DIM ** -0.5)
TOPK_MAX = 256
ROPE_THETA = 10000.0
Q_BLOCK = 128
D_FF = ((8 * D_MODEL // 3 + 255) // 256) * 256
EPS = 1e-6
IN_SIZES = (D_INNER, CONV_DIM, SSD_HEADS, ATTN_WIDTH, KV_WIDTH, KV_WIDTH,
            IDX_HEADS * IDX_DIM, IDX_DIM, IDX_HEADS, D_MODEL, D_MODEL)
N_IN = sum(IN_SIZES)

kernel_name = "hybrid_ssd_dsa_gated_decoder_step"


def rmsnorm(x, g):
    xf = x.astype(jnp.float32)
    y = xf * lax.rsqrt(jnp.mean(xf * xf, axis=-1, keepdims=True) + EPS)
    return (y * g.astype(jnp.float32)).astype(x.dtype)


def layernorm(x, g, b):
    xf = x.astype(jnp.float32)
    mu = jnp.mean(xf, axis=-1, keepdims=True)
    var = jnp.mean((xf - mu) ** 2, axis=-1, keepdims=True)
    y = (xf - mu) * lax.rsqrt(var + EPS) * g.astype(jnp.float32) + b.astype(jnp.float32)
    return y.astype(x.dtype)


def gated_rmsnorm(y, z, g):
    u = (y * jax.nn.silu(z)).astype(jnp.float32)
    shp = u.shape
    u = u.reshape(shp[:-1] + (SSD_GROUPS, shp[-1] // SSD_GROUPS))
    u = u * lax.rsqrt(jnp.mean(u * u, axis=-1, keepdims=True) + EPS)
    return (u.reshape(shp) * g.astype(jnp.float32)).astype(y.dtype)


def rope(x, pos):
    d = x.shape[-1]
    half = d // 2
    inv = ROPE_THETA ** (-(jnp.arange(half, dtype=jnp.float32) * 2.0) / d)
    ang = pos.astype(jnp.float32)[:, None] * inv[None, :]
    cos = jnp.cos(ang)[None, :, None, :]
    sin = jnp.sin(ang)[None, :, None, :]
    x1 = x[..., :half].astype(jnp.float32)
    x2 = x[..., half:].astype(jnp.float32)
    out = jnp.concatenate([x1 * cos - x2 * sin, x2 * cos + x1 * sin], axis=-1)
    return out.astype(x.dtype)


def split_columns(u):
    cuts = tuple(int(c) for c in np.cumsum(IN_SIZES)[:-1])
    return jnp.split(u, cuts, axis=-1)


def project(x, pos, g_pre, w_in, g_ik, b_ik):
    bsz, t = x.shape[0], x.shape[1]
    h = rmsnorm(x, g_pre)
    u = h @ w_in
    z, xbc, dt_raw, q, k, v, qi, ki, wi, ga, gb = split_columns(u)
    q = rope(q.reshape(bsz, t, N_HEADS, HEAD_DIM), pos)
    k = rope(k.reshape(bsz, t, N_KV_HEADS, HEAD_DIM), pos)
    v = v.reshape(bsz, t, N_KV_HEADS, HEAD_DIM)
    qi = rope(qi.reshape(bsz, t, IDX_HEADS, IDX_DIM), pos)
    ki = rope(layernorm(ki, g_ik, b_ik)[:, :, None, :], pos)[:, :, 0, :]
    return z, xbc, dt_raw, q, k, v, qi, ki, wi, ga, gb


def causal_conv(xbc, buf, w, b):
    t = xbc.shape[1]
    xp = jnp.concatenate([buf.astype(xbc.dtype), xbc], axis=1)
    out = b
    for j in range(CONV_W):
        out = out + w[j] * xp[:, j:j + t]
    return out, xp[:, -(CONV_W - 1):]


def ssd_prep(xbc_conv, dt_raw, dt_bias, a_log):
    bsz, t = xbc_conv.shape[0], xbc_conv.shape[1]
    xbc = jax.nn.silu(xbc_conv)
    xs, bs, cs = jnp.split(xbc, [D_INNER, D_INNER + SSD_GROUPS * SSD_STATE], axis=-1)
    xs = xs.reshape(bsz, t, SSD_HEADS, SSD_HEAD_DIM)
    bs = bs.reshape(bsz, t, SSD_GROUPS, SSD_STATE)
    cs = cs.reshape(bsz, t, SSD_GROUPS, SSD_STATE)
    dt = jax.nn.softplus((dt_raw + dt_bias).astype(jnp.float32))
    a = -jnp.exp(a_log.astype(jnp.float32))
    return xs, bs, cs, dt, a


def ssd_chunked(xs, dt, a, bs, cs):
    bsz, s = xs.shape[0], xs.shape[1]
    q = SSD_CHUNK
    nc = s // q
    r = SSD_HEADS // SSD_GROUPS
    xr = xs.reshape(bsz, nc, q, SSD_GROUPS, r, SSD_HEAD_DIM)
    dtr = dt.reshape(bsz, nc, q, SSD_GROUPS, r)
    br = bs.reshape(bsz, nc, q, SSD_GROUPS, SSD_STATE)
    cr = cs.reshape(bsz, nc, q, SSD_GROUPS, SSD_STATE)
    cum = jnp.cumsum(dtr * a.reshape(SSD_GROUPS, r), axis=2)
    causal = jnp.tril(jnp.ones((q, q), dtype=bool))[:, :, None, None]
    seg = cum[:, :, :, None] - cum[:, :, None, :]
    decay = jnp.exp(jnp.where(causal, seg, -jnp.inf))
    xdt = xr * dtr[..., None]
    cb = jnp.einsum("bclgn,bcsgn->bclsg", cr, br)
    y_diag = jnp.einsum("bclsg,bclsgr,bcsgrp->bclgrp", cb, decay, xdt)
    decay_last = jnp.exp(cum[:, :, -1:] - cum)
    chunk_states = jnp.einsum("bcsgn,bcsgr,bcsgrp->bcgrpn", br, decay_last, xdt)
    chunk_decay = jnp.exp(cum[:, :, -1])

    def step(h, inp):
        st, dec = inp
        return dec[..., None, None] * h + st, h

    h0 = jnp.zeros((bsz, SSD_GROUPS, r, SSD_HEAD_DIM, SSD_STATE), chunk_states.dtype)
    h_final, h_starts = lax.scan(step, h0, (jnp.moveaxis(chunk_states, 1, 0), jnp.moveaxis(chunk_decay, 1, 0)))
    h_starts = jnp.moveaxis(h_starts, 0, 1)
    y_off = jnp.einsum("bclgn,bcgrpn,bclgr->bclgrp", cr, h_starts, jnp.exp(cum))
    y = (y_diag + y_off).reshape(bsz, s, SSD_HEADS, SSD_HEAD_DIM)
    return y, h_final.reshape(bsz, SSD_HEADS, SSD_HEAD_DIM, SSD_STATE)


def ssd_recurrent(xs, dt, a, bs, cs, h0):
    bsz, t = xs.shape[0], xs.shape[1]
    r = SSD_HEADS // SSD_GROUPS
    ar = a.reshape(SSD_GROUPS, r)
    xr = jnp.moveaxis(xs.reshape(bsz, t, SSD_GROUPS, r, SSD_HEAD_DIM), 1, 0)
    dtr = jnp.moveaxis(dt.reshape(bsz, t, SSD_GROUPS, r), 1, 0)
    br = jnp.moveaxis(bs, 1, 0)
    cr = jnp.moveaxis(cs, 1, 0)

    def step(h, inp):
        xt, dtt, bt, ct = inp
        h = jnp.exp(dtt * ar)[..., None, None] * h + jnp.einsum("bgrp,bgn,bgr->bgrpn", xt, bt, dtt)
        return h, jnp.einsum("bgrpn,bgn->bgrp", h, ct)

    h_init = h0.astype(jnp.float32).reshape(bsz, SSD_GROUPS, r, SSD_HEAD_DIM, SSD_STATE)
    h, ys = lax.scan(step, h_init, (xr, dtr, br, cr))
    y = jnp.moveaxis(ys, 0, 1).reshape(bsz, t, SSD_HEADS, SSD_HEAD_DIM)
    return y, h.reshape(bsz, SSD_HEADS, SSD_HEAD_DIM, SSD_STATE)


def ssd_finish(y, xs, z, d_skip, g_ssd):
    bsz, t = xs.shape[0], xs.shape[1]
    y = y.astype(xs.dtype) + d_skip[:, None].astype(xs.dtype) * xs
    return gated_rmsnorm(y.reshape(bsz, t, D_INNER), z, g_ssd)


def indexer_scores(qi, ki, wi):
    dots = jax.nn.relu(jnp.einsum("bthd,bsd->bths", qi, ki).astype(jnp.float32))
    return jnp.einsum("bths,bth->bts", dots, wi.astype(jnp.float32) * IDX_SCALE)


def attend_selected(q, ks, vs, valid):
    bsz, t = q.shape[0], q.shape[1]
    qg = q.reshape(bsz, t, N_KV_HEADS, N_HEADS // N_KV_HEADS, HEAD_DIM)
    logits = jnp.einsum("btgrd,btkgd->btgrk", qg, ks).astype(jnp.float32) * (HEAD_DIM ** -0.5)
    logits = jnp.where(valid[:, :, None, None, :], logits, -jnp.inf)
    p = jax.nn.softmax(logits, axis=-1).astype(vs.dtype)
    o = jnp.einsum("btgrk,btkgd->btgrd", p, vs)
    return o.reshape(bsz, t, ATTN_WIDTH)


def gather_rows(rows, idx):
    return jax.vmap(lambda rr, ii: rr[ii])(rows, idx)


def dsa_prompt(q, k, v, qi, ki, wi):
    bsz, s = q.shape[0], q.shape[1]
    topk = min(TOPK_MAX, s // 4)
    key_pos = jnp.arange(s)

    def block(start):
        qb = lax.dynamic_slice_in_dim(q, start, Q_BLOCK, axis=1)
        qib = lax.dynamic_slice_in_dim(qi, start, Q_BLOCK, axis=1)
        wib = lax.dynamic_slice_in_dim(wi, start, Q_BLOCK, axis=1)
        qpos = start + jnp.arange(Q_BLOCK)
        sc = indexer_scores(qib, ki, wib)
        sc = jnp.where((key_pos[None, :] <= qpos[:, None])[None], sc, -jnp.inf)
        _, sel = lax.top_k(sc, topk)
        valid = sel <= qpos[None, :, None]
        return attend_selected(qb, gather_rows(k, sel), gather_rows(v, sel), valid)

    out = lax.map(block, jnp.arange(0, s, Q_BLOCK))
    return jnp.moveaxis(out, 0, 1).reshape(bsz, s, ATTN_WIDTH)


def dsa_sample(q, k_new, v_new, qi, ki_new, wi, cache_k, cache_v, cache_idx_k, page_table, layer):
    bsz, t = q.shape[0], q.shape[1]
    n_pages = page_table.shape[1]
    past = n_pages * PAGE_SIZE
    total = past + t
    topk = min(TOPK_MAX, total // 4)
    past_ki = cache_idx_k[layer, page_table].reshape(bsz, past, IDX_DIM)
    all_ki = jnp.concatenate([past_ki.astype(ki_new.dtype), ki_new], axis=1)
    sc = indexer_scores(qi, all_ki, wi)
    qpos = past + jnp.arange(t)
    sc = jnp.where((jnp.arange(total)[None, :] <= qpos[:, None])[None], sc, -jnp.inf)
    _, sel = lax.top_k(sc, topk)
    valid = sel <= qpos[None, :, None]
    in_past = sel < past
    sel_p = jnp.minimum(sel, past - 1)
    phys = jnp.take_along_axis(page_table, (sel_p // PAGE_SIZE).reshape(bsz, -1), axis=1).reshape(sel.shape)
    off = sel_p % PAGE_SIZE
    new_idx = jnp.clip(sel - past, 0, t - 1)
    ks = jnp.where(in_past[..., None, None], cache_k[layer, phys, off].astype(k_new.dtype), gather_rows(k_new, new_idx))
    vs = jnp.where(in_past[..., None, None], cache_v[layer, phys, off].astype(v_new.dtype), gather_rows(v_new, new_idx))
    return attend_selected(q, ks, vs, valid)


def merge_and_ffn(x, y_ssd, y_attn, ga, gb, w_ssd_out, w_attn_out, w_o, g_mix_post, g_ffn_pre, w_gate_up, w_down, g_ffn_post):
    mixed = jax.nn.sigmoid(ga) * (y_ssd @ w_ssd_out) + jax.nn.sigmoid(gb) * (y_attn @ w_attn_out)
    x = x + rmsnorm(mixed @ w_o, g_mix_post)
    gate, up = jnp.split(rmsnorm(x, g_ffn_pre) @ w_gate_up, 2, axis=-1)
    return x + rmsnorm((jax.nn.silu(gate) * up) @ w_down, g_ffn_post)


def setup_inputs(seed: int = 0) -> dict:
    key = jax.random.key(seed)
    ks = jax.random.split(key, 32)
    f32 = jnp.float32
    n_pages = PAST_LEN // PAGE_SIZE
    n_used = DEC_BATCH * n_pages
    n_pool = n_used + n_used // 4

    def nrm(k, shape, scale):
        return scale * jax.random.normal(k, shape, f32)

    dt0 = jnp.exp(jax.random.uniform(ks[14], (DEPTH, SSD_HEADS), f32, math.log(DT_MIN), math.log(DT_MAX)))
    return {
        "x_prompt": nrm(ks[0], (BATCH, SEQ, D_MODEL), 1.0),
        "x_sample": nrm(ks[1], (DEC_BATCH, DEC_SEQ, D_MODEL), 1.0),
        "cache_k": nrm(ks[2], (DEPTH, n_pool, PAGE_SIZE, N_KV_HEADS, HEAD_DIM), 1.0),
        "cache_v": nrm(ks[3], (DEPTH, n_pool, PAGE_SIZE, N_KV_HEADS, HEAD_DIM), 1.0),
        "cache_idx_k": nrm(ks[4], (DEPTH, n_pool, PAGE_SIZE, IDX_DIM), 1.0),
        "state_ssm": nrm(ks[5], (DEPTH, DEC_BATCH, SSD_HEADS, SSD_HEAD_DIM, SSD_STATE), 0.1),
        "state_conv": nrm(ks[6], (DEPTH, DEC_BATCH, CONV_W - 1, CONV_DIM), 1.0),
        "page_table": jax.random.permutation(ks[7], n_pool)[:n_used].reshape(DEC_BATCH, n_pages).astype(jnp.int32),
        "g_mix_pre": 1.0 + nrm(ks[8], (DEPTH, D_MODEL), 0.05),
        "w_in": nrm(ks[9], (DEPTH, D_MODEL, N_IN), D_MODEL ** -0.5),
        "g_idx_k": 1.0 + nrm(ks[10], (DEPTH, IDX_DIM), 0.05),
        "b_idx_k": nrm(ks[11], (DEPTH, IDX_DIM), 0.01),
        "conv_w": nrm(ks[12], (DEPTH, CONV_W, CONV_DIM), CONV_W ** -0.5),
        "conv_b": nrm(ks[13], (DEPTH, CONV_DIM), 0.01),
        "dt_bias": dt0 + jnp.log(-jnp.expm1(-dt0)),
        "a_log": jnp.log(jax.random.uniform(ks[15], (DEPTH, SSD_HEADS), f32, 1.0, 16.0)),
        "d_skip": 1.0 + nrm(ks[16], (DEPTH, SSD_HEADS), 0.1),
        "g_ssd": 1.0 + nrm(ks[17], (DEPTH, D_INNER), 0.05),
        "w_ssd_out": nrm(ks[18], (DEPTH, D_INNER, D_MODEL), D_INNER ** -0.5),
        "w_attn_out": nrm(ks[19], (DEPTH, ATTN_WIDTH, D_MODEL), ATTN_WIDTH ** -0.5),
        "w_o": nrm(ks[20], (DEPTH, D_MODEL, D_MODEL), D_MODEL ** -0.5),
        "g_mix_post": 1.0 + nrm(ks[21], (DEPTH, D_MODEL), 0.05),
        "g_ffn_pre": 1.0 + nrm(ks[22], (DEPTH, D_MODEL), 0.05),
        "w_gate_up": nrm(ks[23], (DEPTH, D_MODEL, 2 * D_FF), D_MODEL ** -0.5),
        "w_down": nrm(ks[24], (DEPTH, D_FF, D_MODEL), D_FF ** -0.5),
        "g_ffn_post": 1.0 + nrm(ks[25], (DEPTH, D_MODEL), 0.05),
    }


def reference(x_prompt, x_sample, cache_k, cache_v, cache_idx_k, state_ssm, state_conv, page_table,
              g_mix_pre, w_in, g_idx_k, b_idx_k, conv_w, conv_b, dt_bias, a_log, d_skip, g_ssd,
              w_ssd_out, w_attn_out, w_o, g_mix_post, g_ffn_pre, w_gate_up, w_down, g_ffn_post):
    bp, sp = x_prompt.shape[0], x_prompt.shape[1]
    ts = x_sample.shape[1]
    past = page_table.shape[1] * PAGE_SIZE
    pos_p = jnp.arange(sp)
    pos_s = past + jnp.arange(ts)
    yp, ys = x_prompt, x_sample
    kp_l, vp_l, ikp_l, ssmp_l, convp_l = [], [], [], [], []
    ks_l, vs_l, iks_l, ssms_l, convs_l = [], [], [], [], []
    for l in range(DEPTH):
        z, xbc, dtr, q, k, v, qi, ki, wi, ga, gb = project(yp, pos_p, g_mix_pre[l], w_in[l], g_idx_k[l], b_idx_k[l])
        xbc_c, conv_p = causal_conv(xbc, jnp.zeros((bp, CONV_W - 1, CONV_DIM), xbc.dtype), conv_w[l], conv_b[l])
        xs, bs, cs, dt, a = ssd_prep(xbc_c, dtr, dt_bias[l], a_log[l])
        y_s, ssm_p = ssd_chunked(xs, dt, a, bs, cs)
        y_ssd = ssd_finish(y_s, xs, z, d_skip[l], g_ssd[l])
        y_attn = dsa_prompt(q, k, v, qi, ki, wi)
        yp = merge_and_ffn(yp, y_ssd, y_attn, ga, gb, w_ssd_out[l], w_attn_out[l], w_o[l],
                           g_mix_post[l], g_ffn_pre[l], w_gate_up[l], w_down[l], g_ffn_post[l])
        kp_l.append(k)
        vp_l.append(v)
        ikp_l.append(ki)
        ssmp_l.append(ssm_p.astype(state_ssm.dtype))
        convp_l.append(conv_p.astype(state_conv.dtype))
        z, xbc, dtr, q, k, v, qi, ki, wi, ga, gb = project(ys, pos_s, g_mix_pre[l], w_in[l], g_idx_k[l], b_idx_k[l])
        xbc_c, conv_s = causal_conv(xbc, state_conv[l], conv_w[l], conv_b[l])
        xs, bs, cs, dt, a = ssd_prep(xbc_c, dtr, dt_bias[l], a_log[l])
        y_s, ssm_s = ssd_recurrent(xs, dt, a, bs, cs, state_ssm[l])
        y_ssd = ssd_finish(y_s, xs, z, d_skip[l], g_ssd[l])
        y_attn = dsa_sample(q, k, v, qi, ki, wi, cache_k, cache_v, cache_idx_k, page_table, l)
        ys = merge_and_ffn(ys, y_ssd, y_attn, ga, gb, w_ssd_out[l], w_attn_out[l], w_o[l],
                           g_mix_post[l], g_ffn_pre[l], w_gate_up[l], w_down[l], g_ffn_post[l])
        ks_l.append(k)
        vs_l.append(v)
        iks_l.append(ki)
        ssms_l.append(ssm_s.astype(state_ssm.dtype))
        convs_l.append(conv_s.astype(state_conv.dtype))
    k_prompt = jnp.stack(kp_l, axis=0)
    v_prompt = jnp.stack(vp_l, axis=0)
    idx_k_prompt = jnp.stack(ikp_l, axis=0)
    ssm_prompt = jnp.stack(ssmp_l, axis=0)
    conv_prompt = jnp.stack(convp_l, axis=0)
    k_sample = jnp.stack(ks_l, axis=0)
    v_sample = jnp.stack(vs_l, axis=0)
    idx_k_sample = jnp.stack(iks_l, axis=0)
    ssm_sample = jnp.stack(ssms_l, axis=0)
    conv_sample = jnp.stack(convs_l, axis=0)
    return (yp, ys, k_prompt, v_prompt, idx_k_prompt, ssm_prompt, conv_prompt,
            k_sample, v_sample, idx_k_sample, ssm_sample, conv_sample)
```

```python
import functools
import math

import numpy as np
import jax
import jax.numpy as jnp
from jax import lax
from jax.experimental import pallas as pl
from jax.experimental.pallas import tpu as pltpu

F32 = jnp.float32
BF16 = jnp.bfloat16
I32 = jnp.int32

D_MODEL = 1024
D_INNER = 2048
SSD_HEAD_DIM = 64
SSD_HEADS = 32
SSD_GROUPS = 4
SSD_STATE = 128
CONV_W = 4
CONV_DIM = D_INNER + 2 * SSD_GROUPS * SSD_STATE
SSD_CHUNK = 128
N_HEADS = 16
N_KV_HEADS = 4
HEAD_DIM = 64
ATTN_WIDTH = N_HEADS * HEAD_DIM
KV_WIDTH = N_KV_HEADS * HEAD_DIM
IDX_HEADS = 8
IDX_DIM = 64
IDX_SCALE = (IDX_HEADS ** -0.5) * (IDX_DIM ** -0.5)
TOPK_MAX = 256
ROPE_THETA = 10000.0
PAGE_SIZE = 128
D_FF = 2816
EPS = 1e-6

LANES = 128
SUBLANES = 8
VMEM_LIMIT = 60 * 1024 * 1024

Z0 = 0
XBC0 = Z0 + D_INNER
Q0 = XBC0 + CONV_DIM
K0 = Q0 + ATTN_WIDTH
V0 = K0 + KV_WIDTH
QI0 = V0 + KV_WIDTH
GA0 = QI0 + IDX_HEADS * IDX_DIM
GB0 = GA0 + D_MODEL
SM0 = GB0 + D_MODEL
N_PROJ = SM0 + LANES
SM_DT = 0
SM_WI = SSD_HEADS
SM_KI = 64

NEG_BIG = -1e30
INT_MIN = -(2 ** 31)
KEY_BLOCK = 512
PAGES_PER_STEP = 16


def _dot(a, b):
    return jnp.dot(a, b, preferred_element_type=F32)


def _dot_nt(a, b):
    return lax.dot_general(a, b, (((1,), (1,)), ((), ())), preferred_element_type=F32)


def _sigmoid(x):
    return 1.0 / (1.0 + jnp.exp(-x))


def _silu(x):
    return x * _sigmoid(x)


def _softplus(x):
    return jnp.maximum(x, 0.0) + jnp.log1p(jnp.exp(-jnp.abs(x)))


def _rmsnorm(x, g):
    return x * lax.rsqrt(jnp.mean(x * x, axis=-1, keepdims=True) + EPS) * g


def _rope_tile(x, cos, sin_signed, first_half):
    partner = jnp.where(first_half, pltpu.roll(x, LANES - 32, 1), pltpu.roll(x, 32, 1))
    return x * cos + partner * sin_signed


def _rope_wide(x, cos, sin_signed, first_half):
    n = x.shape[1] // LANES
    return jnp.concatenate(
        [_rope_tile(x[:, c * LANES:(c + 1) * LANES], cos, sin_signed, first_half) for c in range(n)], axis=1)


def _sort_key(x):
    bits = pltpu.bitcast(x, I32)
    return bits ^ ((bits >> 31) & 0x7FFFFFFF)


def _const_spec(shape):
    nd = len(shape)
    return pl.BlockSpec(shape, lambda *_: (0,) * nd, pipeline_mode=pl.Buffered(1))


def _inproj_body(x_ref, cos_ref, sin_ref, g_ref, w_ref, gik_ref, bik_ref,
                 z_ref, xbc_ref, k_ref, v_ref, ga_ref, gb_ref, ki_ref, sm_ref,
                 qb_ref, kb_ref, vb_ref, qib_ref, kib_ref):
    h = _rmsnorm(x_ref[...], g_ref[...]).astype(BF16)
    cos = cos_ref[...]
    sin = sin_ref[...]
    lane = lax.broadcasted_iota(I32, cos.shape, 1)
    first_half = (lane % HEAD_DIM) < (HEAD_DIM // 2)

    def seg(off, n):
        return _dot(h, w_ref[:, off:off + n])

    z_ref[...] = seg(Z0, D_INNER)
    xbc_ref[...] = seg(XBC0, CONV_DIM)
    ga_ref[...] = seg(GA0, D_MODEL)
    gb_ref[...] = seg(GB0, D_MODEL)
    q = _rope_wide(seg(Q0, ATTN_WIDTH), cos, sin, first_half)
    qb_ref[...] = (q * (HEAD_DIM ** -0.5)).astype(BF16)
    k = _rope_wide(seg(K0, KV_WIDTH), cos, sin, first_half)
    k_ref[...] = k
    kb_ref[...] = k.astype(BF16)
    v = seg(V0, KV_WIDTH)
    v_ref[...] = v
    vb_ref[...] = v.astype(BF16)
    qib_ref[...] = _rope_wide(seg(QI0, IDX_HEADS * IDX_DIM), cos, sin, first_half).astype(BF16)
    small = seg(SM0, LANES)
    is_ki = lane >= SM_KI
    mu = jnp.sum(jnp.where(is_ki, small, 0.0), axis=-1, keepdims=True) * (1.0 / IDX_DIM)
    dev = jnp.where(is_ki, small - mu, 0.0)
    var = jnp.sum(dev * dev, axis=-1, keepdims=True) * (1.0 / IDX_DIM)
    kin = dev * lax.rsqrt(var + EPS) * gik_ref[...] + bik_ref[...]
    kir = _rope_tile(kin, cos, sin, first_half)
    ki_ref[...] = kir[:, SM_KI:]
    kib_ref[...] = kir[:, SM_KI:].astype(BF16)
    is_wi = (lane >= SM_WI) & (lane < SM_WI + IDX_HEADS)
    sm_ref[...] = jnp.where(is_wi, small * IDX_SCALE, small)


def _in_projection(x2d, cos_tab, sin_tab, tab_index, g, w_perm, gik, bik, tm):
    m = x2d.shape[0]
    row = lambda n: pl.BlockSpec((tm, n), lambda i: (i, 0))
    out_shapes = [
        jax.ShapeDtypeStruct((m, D_INNER), F32),
        jax.ShapeDtypeStruct((m, CONV_DIM), F32),
        jax.ShapeDtypeStruct((m, KV_WIDTH), F32),
        jax.ShapeDtypeStruct((m, KV_WIDTH), F32),
        jax.ShapeDtypeStruct((m, D_MODEL), F32),
        jax.ShapeDtypeStruct((m, D_MODEL), F32),
        jax.ShapeDtypeStruct((m, IDX_DIM), F32),
        jax.ShapeDtypeStruct((m, LANES), F32),
        jax.ShapeDtypeStruct((m, ATTN_WIDTH), BF16),
        jax.ShapeDtypeStruct((m, KV_WIDTH), BF16),
        jax.ShapeDtypeStruct((m, KV_WIDTH), BF16),
        jax.ShapeDtypeStruct((m, IDX_HEADS * IDX_DIM), BF16),
        jax.ShapeDtypeStruct((m, IDX_DIM), BF16),
    ]
    return pl.pallas_call(
        _inproj_body,
        grid=(m // tm,),
        in_specs=[
            row(D_MODEL),
            pl.BlockSpec((tm, LANES), tab_index),
            pl.BlockSpec((tm, LANES), tab_index),
            _const_spec((1, D_MODEL)),
            _const_spec((D_MODEL, N_PROJ)),
            _const_spec((1, LANES)),
            _const_spec((1, LANES)),
        ],
        out_specs=[row(s.shape[1]) for s in out_shapes],
        out_shape=out_shapes,
        compiler_params=pltpu.CompilerParams(dimension_semantics=("arbitrary",), vmem_limit_bytes=VMEM_LIMIT),
        name="in_projection",
    )(x2d, cos_tab, sin_tab, g, w_perm, gik, bik)


def _pair_cols(col_a, col_b, first):
    return jnp.where(first, col_a, col_b)


def _ssd_prompt_body(xbc_ref, z_ref, sm_ref, cw_ref, cbias_ref, dtb_ref, a_ref, dskip_ref, gssd_ref,
                     y_ref, ssm_ref, tail_ref, xp_scr, st_scr, y_scr):
    c = pl.program_id(1)
    q = SSD_CHUNK

    @pl.when(c == 0)
    def _():
        xp_scr[0:SUBLANES, :] = jnp.zeros((SUBLANES, CONV_DIM), F32)
        st_scr[...] = jnp.zeros_like(st_scr)

    xb = xbc_ref[...]
    xp_scr[SUBLANES:SUBLANES + q, :] = xb
    conv = cbias_ref[...]
    for j in range(CONV_W - 1):
        lo = SUBLANES - (CONV_W - 1) + j
        conv = conv + cw_ref[j:j + 1, :] * xp_scr[lo:lo + q, :]
    conv = conv + cw_ref[CONV_W - 1:CONV_W, :] * xb
    xp_scr[SUBLANES - (CONV_W - 1):SUBLANES, :] = xb[q - (CONV_W - 1):q, :]
    tail_ref[0] = xb[q - SUBLANES:q, :]

    act = _silu(conv)
    xs = act[:, :D_INNER]
    gs = SSD_GROUPS * SSD_STATE
    bs = act[:, D_INNER:D_INNER + gs]
    cs = act[:, D_INNER + gs:]

    lane = lax.broadcasted_iota(I32, (q, LANES), 1)
    rowi = lax.broadcasted_iota(I32, (q, LANES), 0)
    causal = rowi >= lane
    first = lane < SSD_HEAD_DIM
    dt = jnp.where(lane < SSD_HEADS, _softplus(sm_ref[...] + dtb_ref[...]), 0.0)
    dta = dt * -jnp.exp(a_ref[...])
    tril = jnp.where(causal, 1.0, 0.0).astype(F32)
    cum = jnp.dot(tril, dta, preferred_element_type=F32, precision=lax.Precision.HIGHEST)
    cum_t = cum.T
    cum_last = cum[q - 1:q, :]
    e_cum = jnp.exp(cum)
    d_last = jnp.exp(cum_last - cum)
    e_last = jnp.exp(cum_last)

    rpg = SSD_HEADS // SSD_GROUPS
    for g in range(SSD_GROUPS):
        cs_g = cs[:, g * SSD_STATE:(g + 1) * SSD_STATE].astype(BF16)
        bs_g = bs[:, g * SSD_STATE:(g + 1) * SSD_STATE]
        cb = _dot_nt(cs_g, bs_g.astype(BF16))
        bs_t = bs_g.T.astype(BF16)
        for pr in range(rpg // 2):
            h0 = g * rpg + 2 * pr
            h1 = h0 + 1
            hp = h0 // 2
            xs_p = xs[:, hp * LANES:(hp + 1) * LANES]
            xdt = xs_p * _pair_cols(dt[:, h0:h0 + 1], dt[:, h1:h1 + 1], first)
            xdt_b = xdt.astype(BF16)
            yd = []
            for h in (h0, h1):
                seg = jnp.where(causal, cum[:, h:h + 1] - cum_t[h:h + 1, :], -jnp.inf)
                m = (cb * jnp.exp(seg)).astype(BF16)
                yd.append(_dot(m, xdt_b))
            st = st_scr[hp]
            y_off = _dot(cs_g, st.astype(BF16)) * _pair_cols(e_cum[:, h0:h0 + 1], e_cum[:, h1:h1 + 1], first)
            y_scr[:, hp * LANES:(hp + 1) * LANES] = jnp.where(first, yd[0], yd[1]) + y_off
            xdl = xdt * _pair_cols(d_last[:, h0:h0 + 1], d_last[:, h1:h1 + 1], first)
            dec = _pair_cols(e_last[:, h0:h0 + 1], e_last[:, h1:h1 + 1], first[0:1, :])
            st_scr[hp] = dec * st + _dot(bs_t, xdl.astype(BF16))

    y = y_scr[...] + dskip_ref[...] * xs
    u = y * _silu(z_ref[...])
    gw = D_INNER // SSD_GROUPS
    for g in range(SSD_GROUPS):
        ug = u[:, g * gw:(g + 1) * gw]
        y_ref[:, g * gw:(g + 1) * gw] = _rmsnorm(ug, gssd_ref[:, g * gw:(g + 1) * gw])

    @pl.when(c == pl.num_programs(1) - 1)
    def _():
        for hp in range(SSD_HEADS // 2):
            st_t = st_scr[hp].T
            ssm_ref[0, 2 * hp] = st_t[:SSD_HEAD_DIM, :]
            ssm_ref[0, 2 * hp + 1] = st_t[SSD_HEAD_DIM:, :]


def _ssd_prompt(xbc, z, small, conv_w, conv_b, dtb_pad, a_pad, dskip_full, g_ssd, bsz, seq):
    nc = seq // SSD_CHUNK
    q = SSD_CHUNK
    tok = lambda n: pl.BlockSpec((q, n), lambda b, c: (b * nc + c, 0))
    return pl.pallas_call(
        _ssd_prompt_body,
        grid=(bsz, nc),
        in_specs=[
            tok(CONV_DIM), tok(D_INNER), tok(LANES),
            _const_spec((CONV_W, CONV_DIM)), _const_spec((1, CONV_DIM)),
            _const_spec((1, LANES)), _const_spec((1, LANES)),
            _const_spec((1, D_INNER)), _const_spec((1, D_INNER)),
        ],
        out_specs=[
            tok(D_INNER),
            pl.BlockSpec((1, SSD_HEADS, SSD_HEAD_DIM, SSD_STATE), lambda b, c: (b, 0, 0, 0)),
            pl.BlockSpec((1, SUBLANES, CONV_DIM), lambda b, c: (b, 0, 0)),
        ],
        out_shape=[
            jax.ShapeDtypeStruct((bsz * seq, D_INNER), F32),
            jax.ShapeDtypeStruct((bsz, SSD_HEADS, SSD_HEAD_DIM, SSD_STATE), F32),
            jax.ShapeDtypeStruct((bsz, SUBLANES, CONV_DIM), F32),
        ],
        scratch_shapes=[
            pltpu.VMEM((SUBLANES + q, CONV_DIM), F32),
            pltpu.VMEM((SSD_HEADS // 2, SSD_STATE, LANES), F32),
            pltpu.VMEM((q, D_INNER), F32),
        ],
        compiler_params=pltpu.CompilerParams(dimension_semantics=("arbitrary", "arbitrary"),
                                             vmem_limit_bytes=VMEM_LIMIT),
        name="ssd_prompt",
    )(xbc, z, small, conv_w, conv_b, dtb_pad, a_pad, dskip_full, g_ssd)


def _kth_largest_key(key_scr, nkb, k, rows, extra_key=None):
    def count_ge(cand):
        def body(kb, cnt):
            return cnt + (key_scr[kb] >= cand).astype(I32)
        cnt = lax.fori_loop(0, nkb, body, jnp.zeros((rows, KEY_BLOCK), I32))
        tot = jnp.sum(cnt, axis=1, keepdims=True)
        if extra_key is not None:
            tot = tot + (extra_key >= cand).astype(I32)
        return tot

    zero = jnp.zeros((rows, 1), I32)
    t0 = jnp.where(count_ge(zero) >= k, zero, jnp.full((rows, 1), INT_MIN, I32))

    def bit_step(j, t):
        cand = t | jnp.left_shift(jnp.int32(1), 30 - j)
        return jnp.where(count_ge(cand) >= k, cand, t)

    return lax.fori_loop(0, 31, bit_step, t0)


def _dsa_prompt_body(qi_ref, sm_ref, q_ref, ki_ref, k_ref, v_ref, o_ref, key_scr, bias_scr, *, topk):
    i = pl.program_id(1)
    qb = 128
    kbs = KEY_BLOCK
    nkb = (i * qb + qb + kbs - 1) // kbs
    qi = qi_ref[...]
    sm = sm_ref[...]
    row_pos = i * qb + lax.broadcasted_iota(I32, (qb, kbs), 0)
    col = lax.broadcasted_iota(I32, (qb, kbs), 1)

    def score_block(kb, carry):
        kik = ki_ref[pl.ds(pl.multiple_of(kb * kbs, kbs), kbs), :]
        acc = jnp.zeros((qb, kbs), F32)
        for h in range(IDX_HEADS):
            d = _dot_nt(qi[:, h * IDX_DIM:(h + 1) * IDX_DIM], kik)
            acc = acc + jnp.maximum(d, 0.0) * sm[:, SM_WI + h:SM_WI + h + 1]
        sc = jnp.where(kb * kbs + col <= row_pos, acc + 0.0, -jnp.inf)
        key_scr[kb] = _sort_key(sc)
        return carry

    lax.fori_loop(0, nkb, score_block, 0)
    thr = _kth_largest_key(key_scr, nkb, topk, qb)

    def bias_block(kb, carry):
        sel = (key_scr[kb] >= thr) & (kb * kbs + col <= row_pos)
        bias_scr[kb] = jnp.where(sel, 0.0, NEG_BIG).astype(F32)
        return carry

    lax.fori_loop(0, nkb, bias_block, 0)

    rep = N_HEADS // N_KV_HEADS
    for g in range(N_KV_HEADS):
        qg = jnp.concatenate(
            [q_ref[:, (g * rep + r) * HEAD_DIM:(g * rep + r + 1) * HEAD_DIM] for r in range(rep)], axis=0)

        def att_block(kb, carry):
            m_old, l_old, acc = carry
            start = pl.multiple_of(kb * kbs, kbs)
            kk = k_ref[pl.ds(start, kbs), g * HEAD_DIM:(g + 1) * HEAD_DIM]
            vv = v_ref[pl.ds(start, kbs), g * HEAD_DIM:(g + 1) * HEAD_DIM]
            lg = _dot_nt(qg, kk).reshape(rep, qb, kbs) + bias_scr[kb][None]
            lg = lg.reshape(rep * qb, kbs)
            m_new = jnp.maximum(m_old, jnp.max(lg, axis=1, keepdims=True))
            alpha = jnp.exp(m_old - m_new)
            p = jnp.exp(lg - m_new)
            l_new = alpha * l_old + jnp.sum(p, axis=1, keepdims=True)
            acc = alpha * acc + _dot(p.astype(BF16), vv)
            return m_new, l_new, acc

        init = (jnp.full((rep * qb, 1), NEG_BIG, F32), jnp.zeros((rep * qb, 1), F32),
                jnp.zeros((rep * qb, HEAD_DIM), F32))
        _, l_fin, acc = lax.fori_loop(0, nkb, att_block, init)
        out = acc / l_fin
        for r in range(rep):
            hh = g * rep + r
            o_ref[:, hh * HEAD_DIM:(hh + 1) * HEAD_DIM] = out[r * qb:(r + 1) * qb, :]


def _dsa_prompt(qib, small, qb16, kib, kb16, vb16, bsz, seq):
    topk = min(TOPK_MAX, seq // 4)
    qblk = 128
    nq = seq // qblk
    tok = lambda n: pl.BlockSpec((qblk, n), lambda b, i: (b * nq + i, 0))
    per_seq = lambda n: pl.BlockSpec((seq, n), lambda b, i: (b, 0))
    nblocks = seq // KEY_BLOCK
    return pl.pallas_call(
        functools.partial(_dsa_prompt_body, topk=topk),
        grid=(bsz, nq),
        in_specs=[tok(IDX_HEADS * IDX_DIM), tok(LANES), tok(ATTN_WIDTH),
                  per_seq(IDX_DIM), per_seq(KV_WIDTH), per_seq(KV_WIDTH)],
        out_specs=tok(ATTN_WIDTH),
        out_shape=jax.ShapeDtypeStruct((bsz * seq, ATTN_WIDTH), F32),
        scratch_shapes=[pltpu.VMEM((nblocks, qblk, KEY_BLOCK), I32),
                        pltpu.VMEM((nblocks, qblk, KEY_BLOCK), F32)],
        compiler_params=pltpu.CompilerParams(dimension_semantics=("arbitrary", "arbitrary"),
                                             vmem_limit_bytes=VMEM_LIMIT),
        name="dsa_prompt",
    )(qib, small, qb16, kib, kb16, vb16)


def _merge_ffn_body(x_ref, ys_ref, ya_ref, ga_ref, gb_ref, wso_ref, wao_ref, wo_ref, gmp_ref, gfp_ref,
                    wgu_ref, wd_ref, gfo_ref, o_ref):
    mixed = (_sigmoid(ga_ref[...]) * _dot(ys_ref[...].astype(BF16), wso_ref[...])
             + _sigmoid(gb_ref[...]) * _dot(ya_ref[...].astype(BF16), wao_ref[...]))
    x1 = x_ref[...] + _rmsnorm(_dot(mixed.astype(BF16), wo_ref[...]), gmp_ref[...])
    h2 = _rmsnorm(x1, gfp_ref[...]).astype(BF16)
    gate = _dot(h2, wgu_ref[:, :D_FF])
    up = _dot(h2, wgu_ref[:, D_FF:])
    act = (_silu(gate) * up).astype(BF16)
    o_ref[...] = x1 + _rmsnorm(_dot(act, wd_ref[...]), gfo_ref[...])


def _merge_ffn(x2d, y_ssd, y_attn, ga, gb, wso, wao, wo, gmp, gfp, wgu, wd, gfo, tm):
    m = x2d.shape[0]
    row = lambda n: pl.BlockSpec((tm, n), lambda i: (i, 0))
    return pl.pallas_call(
        _merge_ffn_body,
        grid=(m // tm,),
        in_specs=[row(D_MODEL), row(D_INNER), row(ATTN_WIDTH), row(D_MODEL), row(D_MODEL),
                  _const_spec((D_INNER, D_MODEL)), _const_spec((ATTN_WIDTH, D_MODEL)),
                  _const_spec((D_MODEL, D_MODEL)), _const_spec((1, D_MODEL)), _const_spec((1, D_MODEL)),
                  _const_spec((D_MODEL, 2 * D_FF)), _const_spec((D_FF, D_MODEL)), _const_spec((1, D_MODEL))],
        out_specs=row(D_MODEL),
        out_shape=jax.ShapeDtypeStruct((m, D_MODEL), F32),
        compiler_params=pltpu.CompilerParams(dimension_semantics=("arbitrary",), vmem_limit_bytes=VMEM_LIMIT),
        name="merge_ffn",
    )(x2d, y_ssd, y_attn, ga, gb, wso, wao, wo, gmp, gfp, wgu, wd, gfo)


def _ssd_sample_body(xbc_ref, z_ref, sm_ref, sconv_ref, cw_ref, cbias_ref, dtb_ref, a_ref, dskip_ref,
                     gssd_ref, st_ref, y_ref, conv_ref, sto_ref,
                     xs_scr, bs_scr, cs_scr, xt_scr, dtt_scr, dect_scr, yt_scr):
    b = pl.program_id(0)
    nb = pl.num_programs(0)
    hd = SSD_HEAD_DIM
    gs = SSD_GROUPS * SSD_STATE

    @pl.when(b == 0)
    def _():
        xb = xbc_ref[...]
        conv = cbias_ref[...]
        for j in range(CONV_W - 1):
            conv = conv + cw_ref[j:j + 1, :] * sconv_ref[j]
        conv = conv + cw_ref[CONV_W - 1:CONV_W, :] * xb
        for j in range(CONV_W - 2):
            conv_ref[j] = sconv_ref[j + 1]
        conv_ref[CONV_W - 2] = xb
        act = _silu(conv)
        xs = act[:, :D_INNER]
        xs_scr[...] = xs
        bs_scr[...] = act[:, D_INNER:D_INNER + gs]
        cs_scr[...] = act[:, D_INNER + gs:]
        dt = _softplus(sm_ref[...] + dtb_ref[...])
        dec = jnp.exp(dt * -jnp.exp(a_ref[...]))
        dt_full = jnp.concatenate(
            [jnp.broadcast_to(dt[:, h:h + 1], (dt.shape[0], hd)) for h in range(SSD_HEADS)], axis=1)
        dec_full = jnp.concatenate(
            [jnp.broadcast_to(dec[:, h:h + 1], (dt.shape[0], hd)) for h in range(SSD_HEADS)], axis=1)
        xt_scr[...] = xs.T
        dtt_scr[...] = dt_full.T
        dect_scr[...] = dec_full.T
        yt_scr[...] = jnp.zeros_like(yt_scr)

    nbl = xt_scr.shape[1]
    lane_b = lax.broadcasted_iota(I32, (D_INNER, nbl), 1) == b

    def pick(ref):
        return jnp.sum(jnp.where(lane_b, ref[...], 0.0), axis=1, keepdims=True)

    x_col = pick(xt_scr)
    dt_col = pick(dtt_scr)
    dec_col = pick(dect_scr)
    b_row = bs_scr[pl.ds(b, 1), :]
    c_row = cs_scr[pl.ds(b, 1), :]
    rows_pg = (SSD_HEADS // SSD_GROUPS) * hd
    y_cols = []
    for g in range(SSD_GROUPS):
        r0 = g * rows_pg
        hst = st_ref[0, r0:r0 + rows_pg, :]
        bg = b_row[:, g * SSD_STATE:(g + 1) * SSD_STATE]
        cg = c_row[:, g * SSD_STATE:(g + 1) * SSD_STATE]
        hn = dec_col[r0:r0 + rows_pg] * hst + (x_col[r0:r0 + rows_pg] * bg) * dt_col[r0:r0 + rows_pg]
        sto_ref[0, r0:r0 + rows_pg, :] = hn
        y_cols.append(jnp.sum(hn * cg, axis=1, keepdims=True))
    y_col = jnp.concatenate(y_cols, axis=0)
    yt_scr[...] = jnp.where(lane_b, y_col, yt_scr[...])

    @pl.when(b == nb - 1)
    def _():
        y = yt_scr[...].T + dskip_ref[...] * xs_scr[...]
        u = y * _silu(z_ref[...])
        gw = D_INNER // SSD_GROUPS
        for g in range(SSD_GROUPS):
            ug = u[:, g * gw:(g + 1) * gw]
            y_ref[:, g * gw:(g + 1) * gw] = _rmsnorm(ug, gssd_ref[:, g * gw:(g + 1) * gw])


def _ssd_sample(xbc, z, small, sconv, conv_w, conv_b, dtb_pad, a_pad, dskip_full, g_ssd, state):
    nb = xbc.shape[0]
    full = lambda shape: pl.BlockSpec(shape, lambda b: (0,) * len(shape))
    rows = SSD_HEADS * SSD_HEAD_DIM
    st_spec = pl.BlockSpec((1, rows, SSD_STATE), lambda b: (b, 0, 0))
    return pl.pallas_call(
        _ssd_sample_body,
        grid=(nb,),
        in_specs=[full((nb, CONV_DIM)), full((nb, D_INNER)), full((nb, LANES)),
                  full((CONV_W - 1, nb, CONV_DIM)),
                  full((CONV_W, CONV_DIM)), full((1, CONV_DIM)), full((1, LANES)), full((1, LANES)),
                  full((1, D_INNER)), full((1, D_INNER)), st_spec],
        out_specs=[full((nb, D_INNER)), full((CONV_W - 1, nb, CONV_DIM)), st_spec],
        out_shape=[jax.ShapeDtypeStruct((nb, D_INNER), F32),
                   jax.ShapeDtypeStruct((CONV_W - 1, nb, CONV_DIM), F32),
                   jax.ShapeDtypeStruct((nb, rows, SSD_STATE), F32)],
        scratch_shapes=[pltpu.VMEM((nb, D_INNER), F32),
                        pltpu.VMEM((nb, SSD_GROUPS * SSD_STATE), F32),
                        pltpu.VMEM((nb, SSD_GROUPS * SSD_STATE), F32),
                        pltpu.VMEM((D_INNER, nb), F32), pltpu.VMEM((D_INNER, nb), F32),
                        pltpu.VMEM((D_INNER, nb), F32), pltpu.VMEM((D_INNER, nb), F32)],
        compiler_params=pltpu.CompilerParams(dimension_semantics=("arbitrary",), vmem_limit_bytes=VMEM_LIMIT),
        name="ssd_sample",
    )(xbc, z, small, sconv, conv_w, conv_b, dtb_pad, a_pad, dskip_full, g_ssd, state)


def _page_specs(width, pages):
    def spec(j):
        return pl.BlockSpec((1, width, PAGE_SIZE), lambda b, c, pt: (pt[b, c * pages + j], 0, 0))
    return [spec(j) for j in range(pages)]


def _idx_scores_body(pt_ref, qi_ref, wi_ref, *rest):
    pages = rest[:-1]
    o_ref = rest[-1]
    keys_t = jnp.concatenate([p[0] for p in pages], axis=1).astype(BF16)
    d = _dot(qi_ref[0], keys_t)
    o_ref[0] = jnp.sum(jnp.maximum(d, 0.0) * wi_ref[0], axis=0, keepdims=True)


def _idx_scores(page_table, qi3, wi3, cache_idx, pages):
    nb, n_pages = page_table.shape
    nch = n_pages // pages
    span = pages * PAGE_SIZE
    grid_spec = pltpu.PrefetchScalarGridSpec(
        num_scalar_prefetch=1,
        grid=(nb, nch),
        in_specs=[pl.BlockSpec((1, IDX_HEADS, IDX_DIM), lambda b, c, pt: (b, 0, 0)),
                  pl.BlockSpec((1, IDX_HEADS, 1), lambda b, c, pt: (b, 0, 0))]
                 + _page_specs(IDX_DIM, pages),
        out_specs=pl.BlockSpec((1, 1, span), lambda b, c, pt: (b, 0, c)),
    )
    return pl.pallas_call(
        _idx_scores_body,
        grid_spec=grid_spec,
        out_shape=jax.ShapeDtypeStruct((nb, 1, n_pages * PAGE_SIZE), F32),
        compiler_params=pltpu.CompilerParams(dimension_semantics=("arbitrary", "arbitrary"),
                                             vmem_limit_bytes=VMEM_LIMIT),
        name="idx_scores_sample",
    )(page_table, qi3, wi3, *([cache_idx] * pages))


def _select_sample_body(sc_ref, qi_ref, ki_ref, sm_ref, bias_ref, biasn_ref, key_scr, *, topk):
    rows, length = sc_ref.shape
    nkb = length // KEY_BLOCK
    for kb in range(nkb):
        key_scr[kb] = _sort_key(sc_ref[:, kb * KEY_BLOCK:(kb + 1) * KEY_BLOCK] + 0.0)
    ki = ki_ref[...].astype(F32)
    sm = sm_ref[...]
    sc_new = jnp.zeros((rows, 1), F32)
    for h in range(IDX_HEADS):
        d = jnp.sum(qi_ref[:, h * IDX_DIM:(h + 1) * IDX_DIM].astype(F32) * ki, axis=1, keepdims=True)
        sc_new = sc_new + jnp.maximum(d, 0.0) * sm[:, SM_WI + h:SM_WI + h + 1]
    key_new = _sort_key(sc_new + 0.0)
    thr = _kth_largest_key(key_scr, nkb, topk, rows, extra_key=key_new)
    for kb in range(nkb):
        bias_ref[:, kb * KEY_BLOCK:(kb + 1) * KEY_BLOCK] = jnp.where(key_scr[kb] >= thr, 0.0, NEG_BIG).astype(F32)
    biasn_ref[...] = jnp.broadcast_to(jnp.where(key_new >= thr, 0.0, NEG_BIG).astype(F32), biasn_ref.shape)


def _select_sample(scores, qib, kib, small, topk):
    nb, length = scores.shape
    return pl.pallas_call(
        functools.partial(_select_sample_body, topk=topk),
        out_shape=[jax.ShapeDtypeStruct((nb, length), F32), jax.ShapeDtypeStruct((nb, LANES), F32)],
        scratch_shapes=[pltpu.VMEM((length // KEY_BLOCK, nb, KEY_BLOCK), I32)],
        compiler_params=pltpu.CompilerParams(vmem_limit_bytes=VMEM_LIMIT),
        name="select_sample",
    )(scores, qib, kib, small)


def _attn_sample_body(pt_ref, q_ref, bias_ref, kn_ref, vn_ref, biasn_ref, *rest, pages):
    k_pages = rest[:pages]
    v_pages = rest[pages:2 * pages]
    o_ref = rest[2 * pages]
    m_scr, l_scr, acc_scr = rest[2 * pages + 1:]
    c = pl.program_id(1)
    rep = N_HEADS // N_KV_HEADS

    @pl.when(c == 0)
    def _():
        m_scr[...] = jnp.full(m_scr.shape, NEG_BIG, F32)
        l_scr[...] = jnp.zeros_like(l_scr)
        acc_scr[...] = jnp.zeros_like(acc_scr)

    q = q_ref[0]
    head = lax.broadcasted_iota(I32, (N_HEADS, KV_WIDTH), 0)
    lane = lax.broadcasted_iota(I32, (N_HEADS, KV_WIDTH), 1)
    own = (lane // HEAD_DIM) == (head // rep)
    q_bd = jnp.where(own, jnp.concatenate([q.astype(F32)] * N_KV_HEADS, axis=1), 0.0).astype(BF16)
    kk_t = jnp.concatenate([p[0] for p in k_pages], axis=1).astype(BF16)
    vv_t = jnp.concatenate([p[0] for p in v_pages], axis=1).astype(BF16)
    lg = _dot(q_bd, kk_t) + bias_ref[0]
    m_old = m_scr[...]
    m_new = jnp.maximum(m_old, jnp.max(lg, axis=1, keepdims=True))
    alpha = jnp.exp(m_old - m_new)
    p = jnp.exp(lg - m_new)
    l_scr[...] = alpha * l_scr[...] + jnp.sum(p, axis=1, keepdims=True)
    acc_scr[...] = alpha * acc_scr[...] + _dot_nt(p.astype(BF16), vv_t)
    m_scr[...] = m_new

    @pl.when(c == pl.num_programs(1) - 1)
    def _():
        kn = kn_ref[0].astype(F32)
        lg_n = jnp.sum(q_bd.astype(F32) * kn, axis=1, keepdims=True) + biasn_ref[0][:, 0:1]
        m_o = m_scr[...]
        m_n = jnp.maximum(m_o, lg_n)
        al = jnp.exp(m_o - m_n)
        pn = jnp.exp(lg_n - m_n)
        l_fin = al * l_scr[...] + pn
        acc = al * acc_scr[...] + pn.astype(BF16).astype(F32) * vn_ref[0].astype(F32)
        res = jnp.where(own, acc / l_fin, 0.0)
        out = res[:, 0:HEAD_DIM]
        for g in range(1, N_KV_HEADS):
            out = out + res[:, g * HEAD_DIM:(g + 1) * HEAD_DIM]
        o_ref[0] = out


def _attn_sample(page_table, q3, bias3, kn3, vn3, biasn3, cache_k, cache_v, pages):
    nb, n_pages = page_table.shape
    nch = n_pages // pages
    span = pages * PAGE_SIZE
    per_b = lambda s: pl.BlockSpec((1,) + s, lambda b, c, pt: (b, 0, 0))
    grid_spec = pltpu.PrefetchScalarGridSpec(
        num_scalar_prefetch=1,
        grid=(nb, nch),
        in_specs=[per_b((N_HEADS, HEAD_DIM)),
                  pl.BlockSpec((1, 1, span), lambda b, c, pt: (b, 0, c)),
                  per_b((1, KV_WIDTH)), per_b((1, KV_WIDTH)), per_b((1, LANES))]
                 + _page_specs(KV_WIDTH, pages) + _page_specs(KV_WIDTH, pages),
        out_specs=per_b((N_HEADS, HEAD_DIM)),
        scratch_shapes=[pltpu.VMEM((N_HEADS, 1), F32), pltpu.VMEM((N_HEADS, 1), F32),
                        pltpu.VMEM((N_HEADS, KV_WIDTH), F32)],
    )
    return pl.pallas_call(
        functools.partial(_attn_sample_body, pages=pages),
        grid_spec=grid_spec,
        out_shape=jax.ShapeDtypeStruct((nb, N_HEADS, HEAD_DIM), F32),
        compiler_params=pltpu.CompilerParams(dimension_semantics=("arbitrary", "arbitrary"),
                                             vmem_limit_bytes=VMEM_LIMIT),
        name="attn_sample",
    )(page_table, q3, bias3, kn3, vn3, biasn3, *([cache_k] * pages), *([cache_v] * pages))


def _rope_tables(pos):
    half = HEAD_DIM // 2
    inv = ROPE_THETA ** (-(jnp.arange(half, dtype=F32) * 2.0) / HEAD_DIM)
    ang = pos.astype(F32)[:, None] * inv[None, :]
    cos = jnp.cos(ang)
    sin = jnp.sin(ang)
    cos_t = jnp.concatenate([cos, cos, cos, cos], axis=1)
    sin_t = jnp.concatenate([-sin, sin, -sin, sin], axis=1)
    return cos_t, sin_t


def _permute_w_in(w):
    sizes = (D_INNER, CONV_DIM, SSD_HEADS, ATTN_WIDTH, KV_WIDTH, KV_WIDTH,
             IDX_HEADS * IDX_DIM, IDX_DIM, IDX_HEADS, D_MODEL, D_MODEL)
    cuts = np.concatenate([[0], np.cumsum(sizes)])
    z, xbc, dt, q, k, v, qi, ki, wi, ga, gb = [w[:, int(cuts[j]):int(cuts[j + 1])] for j in range(len(sizes))]
    pad = jnp.zeros((w.shape[0], SM_KI - SM_WI - IDX_HEADS), w.dtype)
    return jnp.concatenate([z, xbc, q, k, v, qi, ga, gb, dt, wi, pad, ki], axis=1).astype(BF16)


def _pad_lanes(v, offset=0):
    out = jnp.zeros((1, LANES), F32)
    return out.at[0, offset:offset + v.shape[0]].set(v.astype(F32))


def kernel(x_prompt, x_sample, cache_k, cache_v, cache_idx_k, state_ssm, state_conv, page_table, g_mix_pre, w_in,
           g_idx_k, b_idx_k, conv_w, conv_b, dt_bias, a_log, d_skip, g_ssd, w_ssd_out, w_attn_out, w_o, g_mix_post,
           g_ffn_pre, w_gate_up, w_down, g_ffn_post):
    bp, sp, _ = x_prompt.shape
    bd, ts, _ = x_sample.shape
    assert ts == 1 and w_in.shape[0] == 1, "one decode token per sample sequence, depth 1"
    n_pages = page_table.shape[1]
    past = n_pages * PAGE_SIZE
    layer = 0

    w_perm = _permute_w_in(w_in[layer])
    g_pre = g_mix_pre[layer][None, :]
    gik = _pad_lanes(g_idx_k[layer], SM_KI)
    bik = _pad_lanes(b_idx_k[layer], SM_KI)
    dtb = _pad_lanes(dt_bias[layer])
    a_pad = _pad_lanes(a_log[layer])
    dskip_full = jnp.repeat(d_skip[layer].astype(F32), SSD_HEAD_DIM)[None, :]
    gssd = g_ssd[layer][None, :]
    cw = conv_w[layer]
    cbias = conv_b[layer][None, :]
    wso = w_ssd_out[layer].astype(BF16)
    wao = w_attn_out[layer].astype(BF16)
    wo = w_o[layer].astype(BF16)
    wgu = w_gate_up[layer].astype(BF16)
    wd = w_down[layer].astype(BF16)
    gmp = g_mix_post[layer][None, :]
    gfp = g_ffn_pre[layer][None, :]
    gfo = g_ffn_post[layer][None, :]

    tm = 256
    xp2 = x_prompt.reshape(bp * sp, D_MODEL)
    cos_p, sin_p = _rope_tables(jnp.arange(sp))
    tiles_per_seq = sp // tm
    (z, xbc, k, v, ga, gb, ki, small, qb16, kb16, vb16, qib, kib) = _in_projection(
        xp2, cos_p, sin_p, lambda i: (i % tiles_per_seq, 0), g_pre, w_perm, gik, bik, tm)
    y_ssd, ssm_p, tail = _ssd_prompt(xbc, z, small, cw, cbias, dtb, a_pad, dskip_full, gssd, bp, sp)
    y_attn = _dsa_prompt(qib, small, qb16, kib, kb16, vb16, bp, sp)
    yp = _merge_ffn(xp2, y_ssd, y_attn, ga, gb, wso, wao, wo, gmp, gfp, wgu, wd, gfo, tm)

    y_prompt = yp.reshape(bp, sp, D_MODEL)
    k_prompt = k.reshape(1, bp, sp, N_KV_HEADS, HEAD_DIM)
    v_prompt = v.reshape(1, bp, sp, N_KV_HEADS, HEAD_DIM)
    idx_k_prompt = ki.reshape(1, bp, sp, IDX_DIM)
    ssm_prompt = ssm_p[None]
    conv_prompt = tail[None, :, SUBLANES - (CONV_W - 1):, :]

    xs2 = x_sample.reshape(bd, D_MODEL)
    cos_s, sin_s = _rope_tables(jnp.full((bd,), past, jnp.int32))
    (z, xbc, k, v, ga, gb, ki, small, qb16, kb16, vb16, qib, kib) = _in_projection(
        xs2, cos_s, sin_s, lambda i: (i, 0), g_pre, w_perm, gik, bik, bd)
    sconv = jnp.transpose(state_conv[layer], (1, 0, 2))
    st_in = state_ssm[layer].reshape(bd, SSD_HEADS * SSD_HEAD_DIM, SSD_STATE)
    y_ssd, conv_s, st_out = _ssd_sample(xbc, z, small, sconv, cw, cbias, dtb, a_pad, dskip_full, gssd, st_in)

    pages = math.gcd(PAGES_PER_STEP, n_pages)
    topk = min(TOPK_MAX, (past + ts) // 4)
    wi3 = small[:, SM_WI:SM_WI + IDX_HEADS].reshape(bd, IDX_HEADS, 1)
    cidx = jnp.transpose(cache_idx_k[layer], (0, 2, 1))
    ck = jnp.transpose(cache_k[layer], (0, 2, 3, 1)).reshape(-1, KV_WIDTH, PAGE_SIZE)
    cv = jnp.transpose(cache_v[layer], (0, 2, 3, 1)).reshape(-1, KV_WIDTH, PAGE_SIZE)
    scores = _idx_scores(page_table, qib.reshape(bd, IDX_HEADS, IDX_DIM), wi3, cidx, pages)
    bias, bias_new = _select_sample(scores.reshape(bd, past), qib, kib, small, topk)
    y_attn = _attn_sample(page_table, qb16.reshape(bd, N_HEADS, HEAD_DIM), bias.reshape(bd, 1, past),
                          kb16.reshape(bd, 1, KV_WIDTH), vb16.reshape(bd, 1, KV_WIDTH),
                          bias_new.reshape(bd, 1, LANES), ck, cv, pages)
    ys = _merge_ffn(xs2, y_ssd, y_attn.reshape(bd, ATTN_WIDTH), ga, gb, wso, wao, wo, gmp, gfp, wgu, wd, gfo, bd)

    y_sample = ys.reshape(bd, ts, D_MODEL)
    k_sample = k.reshape(1, bd, ts, N_KV_HEADS, HEAD_DIM)
    v_sample = v.reshape(1, bd, ts, N_KV_HEADS, HEAD_DIM)
    idx_k_sample = ki.reshape(1, bd, ts, IDX_DIM)
    ssm_sample = st_out.reshape(1, bd, SSD_HEADS, SSD_HEAD_DIM, SSD_STATE)
    conv_sample = jnp.transpose(conv_s, (1, 0, 2))[None]
    return (y_prompt, y_sample, k_prompt, v_prompt, idx_k_prompt, ssm_prompt, conv_prompt,
            k_sample, v_sample, idx_k_sample, ssm_sample, conv_sample)
```

```python
import functools
import math

import numpy as np
import jax
import jax.numpy as jnp
from jax import lax
from jax.experimental import pallas as pl
from jax.experimental.pallas import tpu as pltpu

F32 = jnp.float32
BF16 = jnp.bfloat16
I32 = jnp.int32

D_MODEL = 1024
D_INNER = 2048
SSD_HEAD_DIM = 64
SSD_HEADS = 32
SSD_GROUPS = 4
SSD_STATE = 128
CONV_W = 4
CONV_DIM = D_INNER + 2 * SSD_GROUPS * SSD_STATE
SSD_CHUNK = 128
N_HEADS = 16
N_KV_HEADS = 4
HEAD_DIM = 64
ATTN_WIDTH = N_HEADS * HEAD_DIM
KV_WIDTH = N_KV_HEADS * HEAD_DIM
IDX_HEADS = 8
IDX_DIM = 64
IDX_SCALE = (IDX_HEADS ** -0.5) * (IDX_DIM ** -0.5)
TOPK_MAX = 256
ROPE_THETA = 10000.0
PAGE_SIZE = 128
D_FF = 2816
EPS = 1e-6

LANES = 128
SUBLANES = 8
VMEM_LIMIT = 60 * 1024 * 1024

Z0 = 0
XBC0 = Z0 + D_INNER
Q0 = XBC0 + CONV_DIM
K0 = Q0 + ATTN_WIDTH
V0 = K0 + KV_WIDTH
QI0 = V0 + KV_WIDTH
GA0 = QI0 + IDX_HEADS * IDX_DIM
GB0 = GA0 + D_MODEL
SM0 = GB0 + D_MODEL
N_PROJ = SM0 + LANES
SM_DT = 0
SM_WI = SSD_HEADS
SM_KI = 64

NEG_BIG = -1e30
INT_MIN = -(2 ** 31)
KEY_BLOCK = 512
PROMPT_Q_BLOCK = LANES
PROMPT_SCORE_BLOCK = 512
PROMPT_KEY_BLOCK = 512
V_ROWS = 80
IDX_PAGES_PER_STEP = 64
ATTN_PAGES_PER_STEP = 32
SAMPLE_SUB_PAGES = 16


def _dot(a, b):
    return jnp.dot(a, b, preferred_element_type=F32)


def _dot_nt(a, b):
    return lax.dot_general(a, b, (((1,), (1,)), ((), ())), preferred_element_type=F32)


def _sigmoid(x):
    return 1.0 / (1.0 + jnp.exp(-x))


def _silu(x):
    return x * _sigmoid(x)


def _softplus(x):
    return jnp.maximum(x, 0.0) + jnp.log1p(jnp.exp(-jnp.abs(x)))


def _rmsnorm(x, g):
    return x * lax.rsqrt(jnp.mean(x * x, axis=-1, keepdims=True) + EPS) * g


def _rope_tile(x, cos, sin_signed, first_half):
    partner = jnp.where(first_half, pltpu.roll(x, LANES - 32, 1), pltpu.roll(x, 32, 1))
    return x * cos + partner * sin_signed


def _rope_wide(x, cos, sin_signed, first_half):
    n = x.shape[1] // LANES
    return jnp.concatenate(
        [_rope_tile(x[:, c * LANES:(c + 1) * LANES], cos, sin_signed, first_half) for c in range(n)], axis=1)


def _sort_key(x):
    bits = pltpu.bitcast(x, I32)
    return bits ^ ((bits >> 31) & 0x7FFFFFFF)


def _const_spec(shape):
    nd = len(shape)
    return pl.BlockSpec(shape, lambda *_: (0,) * nd, pipeline_mode=pl.Buffered(1))


def _inproj_body(x_ref, cos_ref, sin_ref, g_ref, w_ref, gik_ref, bik_ref,
                 z_ref, xbc_ref, k_ref, v_ref, ga_ref, gb_ref, ki_ref, sm_ref,
                 qb_ref, kb_ref, vb_ref, qib_ref, kib_ref):
    h = _rmsnorm(x_ref[...], g_ref[...]).astype(BF16)
    cos = cos_ref[...]
    sin = sin_ref[...]
    lane = lax.broadcasted_iota(I32, cos.shape, 1)
    first_half = (lane % HEAD_DIM) < (HEAD_DIM // 2)

    def seg(off, n):
        return _dot(h, w_ref[:, off:off + n])

    z_ref[...] = seg(Z0, D_INNER)
    xbc_ref[...] = seg(XBC0, CONV_DIM)
    ga_ref[...] = seg(GA0, D_MODEL)
    gb_ref[...] = seg(GB0, D_MODEL)
    q = _rope_wide(seg(Q0, ATTN_WIDTH), cos, sin, first_half)
    qb_ref[...] = (q * (HEAD_DIM ** -0.5)).astype(BF16)
    k = _rope_wide(seg(K0, KV_WIDTH), cos, sin, first_half)
    k_ref[...] = k
    kb_ref[...] = k.astype(BF16)
    v = seg(V0, KV_WIDTH)
    v_ref[...] = v
    vb_ref[...] = v.astype(BF16)
    qib_ref[...] = _rope_wide(seg(QI0, IDX_HEADS * IDX_DIM), cos, sin, first_half).astype(BF16)
    small = seg(SM0, LANES)
    is_ki = lane >= SM_KI
    mu = jnp.sum(jnp.where(is_ki, small, 0.0), axis=-1, keepdims=True) * (1.0 / IDX_DIM)
    dev = jnp.where(is_ki, small - mu, 0.0)
    var = jnp.sum(dev * dev, axis=-1, keepdims=True) * (1.0 / IDX_DIM)
    kin = dev * lax.rsqrt(var + EPS) * gik_ref[...] + bik_ref[...]
    kir = _rope_tile(kin, cos, sin, first_half)
    ki_ref[...] = kir[:, SM_KI:]
    kib_ref[...] = kir[:, SM_KI:].astype(BF16)
    is_wi = (lane >= SM_WI) & (lane < SM_WI + IDX_HEADS)
    sm_ref[...] = jnp.where(is_wi, small * IDX_SCALE, small)


def _in_projection(x2d, cos_tab, sin_tab, tab_index, g, w_perm, gik, bik, tm):
    m = x2d.shape[0]
    row = lambda n: pl.BlockSpec((tm, n), lambda i: (i, 0))
    out_shapes = [
        jax.ShapeDtypeStruct((m, D_INNER), F32),
        jax.ShapeDtypeStruct((m, CONV_DIM), F32),
        jax.ShapeDtypeStruct((m, KV_WIDTH), F32),
        jax.ShapeDtypeStruct((m, KV_WIDTH), F32),
        jax.ShapeDtypeStruct((m, D_MODEL), F32),
        jax.ShapeDtypeStruct((m, D_MODEL), F32),
        jax.ShapeDtypeStruct((m, IDX_DIM), F32),
        jax.ShapeDtypeStruct((m, LANES), F32),
        jax.ShapeDtypeStruct((m, ATTN_WIDTH), BF16),
        jax.ShapeDtypeStruct((m, KV_WIDTH), BF16),
        jax.ShapeDtypeStruct((m, KV_WIDTH), BF16),
        jax.ShapeDtypeStruct((m, IDX_HEADS * IDX_DIM), BF16),
        jax.ShapeDtypeStruct((m, IDX_DIM), BF16),
    ]
    return pl.pallas_call(
        _inproj_body,
        grid=(m // tm,),
        in_specs=[
            row(D_MODEL),
            pl.BlockSpec((tm, LANES), tab_index),
            pl.BlockSpec((tm, LANES), tab_index),
            _const_spec((1, D_MODEL)),
            _const_spec((D_MODEL, N_PROJ)),
            _const_spec((1, LANES)),
            _const_spec((1, LANES)),
        ],
        out_specs=[row(s.shape[1]) for s in out_shapes],
        out_shape=out_shapes,
        compiler_params=pltpu.CompilerParams(dimension_semantics=("arbitrary",), vmem_limit_bytes=VMEM_LIMIT),
        name="in_projection",
    )(x2d, cos_tab, sin_tab, g, w_perm, gik, bik)


def _pair_cols(col_a, col_b, first):
    return jnp.where(first, col_a, col_b)


def _ssd_prompt_body(xbc_ref, z_ref, sm_ref, cw_ref, cbias_ref, dtb_ref, a_ref, dskip_ref, gssd_ref,
                     y_ref, ssm_ref, tail_ref, xp_scr, st_scr, y_scr):
    c = pl.program_id(1)
    q = SSD_CHUNK

    @pl.when(c == 0)
    def _():
        xp_scr[0:SUBLANES, :] = jnp.zeros((SUBLANES, CONV_DIM), F32)
        st_scr[...] = jnp.zeros_like(st_scr)

    xb = xbc_ref[...]
    xp_scr[SUBLANES:SUBLANES + q, :] = xb
    conv = cbias_ref[...]
    for j in range(CONV_W - 1):
        lo = SUBLANES - (CONV_W - 1) + j
        conv = conv + cw_ref[j:j + 1, :] * xp_scr[lo:lo + q, :]
    conv = conv + cw_ref[CONV_W - 1:CONV_W, :] * xb
    xp_scr[SUBLANES - (CONV_W - 1):SUBLANES, :] = xb[q - (CONV_W - 1):q, :]
    tail_ref[0] = xb[q - SUBLANES:q, :]

    act = _silu(conv)
    xs = act[:, :D_INNER]
    gs = SSD_GROUPS * SSD_STATE
    bs = act[:, D_INNER:D_INNER + gs]
    cs = act[:, D_INNER + gs:]

    lane = lax.broadcasted_iota(I32, (q, LANES), 1)
    rowi = lax.broadcasted_iota(I32, (q, LANES), 0)
    causal = rowi >= lane
    first = lane < SSD_HEAD_DIM
    dt = jnp.where(lane < SSD_HEADS, _softplus(sm_ref[...] + dtb_ref[...]), 0.0)
    dta = dt * -jnp.exp(a_ref[...])
    tril = jnp.where(causal, 1.0, 0.0).astype(F32)
    cum = jnp.dot(tril, dta, preferred_element_type=F32, precision=lax.Precision.HIGHEST)
    cum_t = cum.T
    cum_last = cum[q - 1:q, :]
    e_cum = jnp.exp(cum)
    d_last = jnp.exp(cum_last - cum)
    e_last = jnp.exp(cum_last)

    rpg = SSD_HEADS // SSD_GROUPS
    for g in range(SSD_GROUPS):
        cs_g = cs[:, g * SSD_STATE:(g + 1) * SSD_STATE].astype(BF16)
        bs_g = bs[:, g * SSD_STATE:(g + 1) * SSD_STATE]
        cb = _dot_nt(cs_g, bs_g.astype(BF16))
        bs_t = bs_g.T.astype(BF16)
        for pr in range(rpg // 2):
            h0 = g * rpg + 2 * pr
            h1 = h0 + 1
            hp = h0 // 2
            xs_p = xs[:, hp * LANES:(hp + 1) * LANES]
            xdt = xs_p * _pair_cols(dt[:, h0:h0 + 1], dt[:, h1:h1 + 1], first)
            xdt_b = xdt.astype(BF16)
            yd = []
            for h in (h0, h1):
                seg = jnp.where(causal, cum[:, h:h + 1] - cum_t[h:h + 1, :], -jnp.inf)
                m = (cb * jnp.exp(seg)).astype(BF16)
                yd.append(_dot(m, xdt_b))
            st = st_scr[hp]
            y_off = _dot(cs_g, st.astype(BF16)) * _pair_cols(e_cum[:, h0:h0 + 1], e_cum[:, h1:h1 + 1], first)
            y_scr[:, hp * LANES:(hp + 1) * LANES] = jnp.where(first, yd[0], yd[1]) + y_off
            xdl = xdt * _pair_cols(d_last[:, h0:h0 + 1], d_last[:, h1:h1 + 1], first)
            dec = _pair_cols(e_last[:, h0:h0 + 1], e_last[:, h1:h1 + 1], first[0:1, :])
            st_scr[hp] = dec * st + _dot(bs_t, xdl.astype(BF16))

    y = y_scr[...] + dskip_ref[...] * xs
    u = y * _silu(z_ref[...])
    gw = D_INNER // SSD_GROUPS
    for g in range(SSD_GROUPS):
        ug = u[:, g * gw:(g + 1) * gw]
        y_ref[:, g * gw:(g + 1) * gw] = _rmsnorm(ug, gssd_ref[:, g * gw:(g + 1) * gw])

    @pl.when(c == pl.num_programs(1) - 1)
    def _():
        for hp in range(SSD_HEADS // 2):
            st_t = st_scr[hp].T
            ssm_ref[0, 2 * hp] = st_t[:SSD_HEAD_DIM, :]
            ssm_ref[0, 2 * hp + 1] = st_t[SSD_HEAD_DIM:, :]


def _ssd_prompt(xbc, z, small, conv_w, conv_b, dtb_pad, a_pad, dskip_full, g_ssd, bsz, seq):
    nc = seq // SSD_CHUNK
    q = SSD_CHUNK
    tok = lambda n: pl.BlockSpec((q, n), lambda b, c: (b * nc + c, 0))
    return pl.pallas_call(
        _ssd_prompt_body,
        grid=(bsz, nc),
        in_specs=[
            tok(CONV_DIM), tok(D_INNER), tok(LANES),
            _const_spec((CONV_W, CONV_DIM)), _const_spec((1, CONV_DIM)),
            _const_spec((1, LANES)), _const_spec((1, LANES)),
            _const_spec((1, D_INNER)), _const_spec((1, D_INNER)),
        ],
        out_specs=[
            tok(D_INNER),
            pl.BlockSpec((1, SSD_HEADS, SSD_HEAD_DIM, SSD_STATE), lambda b, c: (b, 0, 0, 0)),
            pl.BlockSpec((1, SUBLANES, CONV_DIM), lambda b, c: (b, 0, 0)),
        ],
        out_shape=[
            jax.ShapeDtypeStruct((bsz * seq, D_INNER), F32),
            jax.ShapeDtypeStruct((bsz, SSD_HEADS, SSD_HEAD_DIM, SSD_STATE), F32),
            jax.ShapeDtypeStruct((bsz, SUBLANES, CONV_DIM), F32),
        ],
        scratch_shapes=[
            pltpu.VMEM((SUBLANES + q, CONV_DIM), F32),
            pltpu.VMEM((SSD_HEADS // 2, SSD_STATE, LANES), F32),
            pltpu.VMEM((q, D_INNER), F32),
        ],
        compiler_params=pltpu.CompilerParams(dimension_semantics=("arbitrary", "arbitrary"),
                                             vmem_limit_bytes=VMEM_LIMIT),
        name="ssd_prompt",
    )(xbc, z, small, conv_w, conv_b, dtb_pad, a_pad, dskip_full, g_ssd)


def _kth_largest_key(key_scr, nkb, k, rows, extra_key=None):
    def count_ge(cand):
        def body(kb, cnt):
            return cnt + (key_scr[kb] >= cand).astype(I32)
        cnt = lax.fori_loop(0, nkb, body, jnp.zeros((rows, KEY_BLOCK), I32))
        tot = jnp.sum(cnt, axis=1, keepdims=True)
        if extra_key is not None:
            tot = tot + (extra_key >= cand).astype(I32)
        return tot

    zero = jnp.zeros((rows, 1), I32)
    t0 = jnp.where(count_ge(zero) >= k, zero, jnp.full((rows, 1), INT_MIN, I32))

    def bit_step(j, t):
        cand = t | jnp.left_shift(jnp.int32(1), 30 - j)
        return jnp.where(count_ge(cand) >= k, cand, t)

    return lax.fori_loop(0, 31, bit_step, t0)


def _dsa_prompt_body(qi_ref, wit_ref, q_ref, ki_ref, k_ref, vt_ref, o_ref, key_scr, bias_scr, m_scr, acc_scr,
                     *, topk):
    i = pl.program_id(1)
    qb = PROMPT_Q_BLOCK
    sbs = PROMPT_SCORE_BLOCK
    kbs = PROMPT_KEY_BLOCK
    n_keys = i * qb + qb
    nsb = (n_keys + sbs - 1) // sbs
    nkb = (n_keys + kbs - 1) // kbs
    q_pos = i * qb + lax.broadcasted_iota(I32, (sbs, qb), 1)
    k_off = lax.broadcasted_iota(I32, (sbs, qb), 0)
    qi_stack = jnp.concatenate(
        [qi_ref[:, h * IDX_DIM:(h + 1) * IDX_DIM] for h in range(IDX_HEADS)], axis=0)
    w_rows = wit_ref[...]

    def score_block(sb, carry):
        start = pl.multiple_of(sb * sbs, sbs)
        d = _dot_nt(ki_ref[pl.ds(start, sbs), :], qi_stack)
        acc = jnp.zeros((sbs, qb), F32)
        for h in range(IDX_HEADS):
            acc = acc + jnp.maximum(d[:, h * qb:(h + 1) * qb], 0.0) * w_rows[h:h + 1, :]
        sc = jnp.where(start + k_off <= q_pos, acc + 0.0, -jnp.inf)
        key_scr[pl.ds(start, sbs), :] = _sort_key(sc)
        return carry

    lax.fori_loop(0, nsb, score_block, 0)

    lanes_acc = 4 * SUBLANES

    def count_ge(cand):
        def body(sb, acc):
            ge = (key_scr[pl.ds(pl.multiple_of(sb * sbs, sbs), sbs), :] >= cand).astype(I32)
            return acc + jnp.sum(ge.reshape(sbs // lanes_acc, lanes_acc, qb), axis=0)
        acc = lax.fori_loop(0, nsb, body, jnp.zeros((lanes_acc, qb), I32))
        return jnp.sum(acc, axis=0, keepdims=True)

    zero = jnp.zeros((1, qb), I32)
    t0 = jnp.where(count_ge(zero) >= topk, zero, jnp.full((1, qb), INT_MIN, I32))

    def bit_step(j, t):
        cand = t | jnp.left_shift(jnp.int32(1), 30 - j)
        return jnp.where(count_ge(cand) >= topk, cand, t)

    thr = lax.fori_loop(0, 31, bit_step, t0)

    def bias_block(sb, carry):
        start = pl.multiple_of(sb * sbs, sbs)
        sel = (key_scr[pl.ds(start, sbs), :] >= thr) & (start + k_off <= q_pos)
        bias_scr[pl.ds(start, sbs), :] = jnp.where(sel, 0.0, NEG_BIG).astype(F32)
        return carry

    lax.fori_loop(0, nsb, bias_block, 0)

    rep = N_HEADS // N_KV_HEADS
    m_scr[...] = jnp.full(m_scr.shape, NEG_BIG, F32)
    acc_scr[...] = jnp.zeros_like(acc_scr)

    def att_block(kb, carry):
        start = pl.multiple_of(kb * kbs, kbs)
        bias = jnp.concatenate([bias_scr[pl.ds(start, kbs), :]] * rep, axis=1)
        groups = range(N_KV_HEADS)
        m_old = [m_scr[g] for g in groups]
        acc_old = [acc_scr[g] for g in groups]
        lg = []
        for g in groups:
            qg = jnp.concatenate(
                [q_ref[:, (g * rep + r) * HEAD_DIM:(g * rep + r + 1) * HEAD_DIM] for r in range(rep)], axis=0)
            lg.append(_dot_nt(k_ref[pl.ds(start, kbs), g * HEAD_DIM:(g + 1) * HEAD_DIM], qg) + bias)
        m_new = [jnp.maximum(m_old[g], jnp.max(lg[g], axis=0, keepdims=True)) for g in groups]
        p = [jnp.exp(lg[g] - m_new[g]).astype(BF16) for g in groups]
        pv = [_dot(vt_ref[kb, g], p[g]) for g in groups]
        for g in groups:
            acc_scr[g] = jnp.exp(m_old[g] - m_new[g]) * acc_old[g] + pv[g]
            m_scr[g] = m_new[g]
        return carry

    lax.fori_loop(0, nkb, att_block, 0)

    for g in range(N_KV_HEADS):
        acc = acc_scr[g]
        out_t = acc[:HEAD_DIM, :] / acc[HEAD_DIM:HEAD_DIM + 1, :]
        for pr in range(rep // 2):
            pair = jnp.concatenate([out_t[:, (2 * pr) * qb:(2 * pr + 1) * qb],
                                    out_t[:, (2 * pr + 1) * qb:(2 * pr + 2) * qb]], axis=0)
            c0 = (g * rep + 2 * pr) * HEAD_DIM
            o_ref[:, c0:c0 + 2 * HEAD_DIM] = pair.T


def _dsa_prompt(qib, wi_t, qb16, kib, kb16, vt4, bsz, seq):
    topk = min(TOPK_MAX, seq // 4)
    qblk = PROMPT_Q_BLOCK
    kbs = PROMPT_KEY_BLOCK
    nq = seq // qblk
    tok = lambda n: pl.BlockSpec((qblk, n), lambda b, i: (b * nq + i, 0))
    per_seq = lambda n: pl.BlockSpec((seq, n), lambda b, i: (b, 0))
    return pl.pallas_call(
        functools.partial(_dsa_prompt_body, topk=topk),
        grid=(bsz, nq),
        in_specs=[tok(IDX_HEADS * IDX_DIM),
                  pl.BlockSpec((IDX_HEADS, qblk), lambda b, i: (0, b * nq + i)),
                  tok(ATTN_WIDTH),
                  per_seq(IDX_DIM), per_seq(KV_WIDTH),
                  pl.BlockSpec((seq // kbs, N_KV_HEADS, V_ROWS, kbs), lambda b, i: (b, 0, 0, 0))],
        out_specs=tok(ATTN_WIDTH),
        out_shape=jax.ShapeDtypeStruct((bsz * seq, ATTN_WIDTH), F32),
        scratch_shapes=[pltpu.VMEM((seq, qblk), I32), pltpu.VMEM((seq, qblk), F32),
                        pltpu.VMEM((N_KV_HEADS, 1, (N_HEADS // N_KV_HEADS) * qblk), F32),
                        pltpu.VMEM((N_KV_HEADS, V_ROWS, (N_HEADS // N_KV_HEADS) * qblk), F32)],
        compiler_params=pltpu.CompilerParams(dimension_semantics=("arbitrary", "arbitrary"),
                                             vmem_limit_bytes=VMEM_LIMIT),
        name="dsa_prompt",
    )(qib, wi_t, qb16, kib, kb16, vt4)


def _values_feature_major(vb16, kbs):
    m = vb16.shape[0]
    v3 = vb16.reshape(m, N_KV_HEADS, HEAD_DIM)
    ones = jnp.ones((m, N_KV_HEADS, 1), BF16)
    pad = jnp.zeros((m, N_KV_HEADS, V_ROWS - HEAD_DIM - 1), BF16)
    v_aug = jnp.concatenate([v3, ones, pad], axis=2)
    return jnp.transpose(v_aug.reshape(m // kbs, kbs, N_KV_HEADS, V_ROWS), (0, 2, 3, 1))


def _merge_ffn_body(x_ref, ys_ref, ya_ref, ga_ref, gb_ref, wso_ref, wao_ref, wo_ref, gmp_ref, gfp_ref,
                    wgu_ref, wd_ref, gfo_ref, o_ref):
    mixed = (_sigmoid(ga_ref[...]) * _dot(ys_ref[...].astype(BF16), wso_ref[...])
             + _sigmoid(gb_ref[...]) * _dot(ya_ref[...].astype(BF16), wao_ref[...]))
    x1 = x_ref[...] + _rmsnorm(_dot(mixed.astype(BF16), wo_ref[...]), gmp_ref[...])
    h2 = _rmsnorm(x1, gfp_ref[...]).astype(BF16)
    gate = _dot(h2, wgu_ref[:, :D_FF])
    up = _dot(h2, wgu_ref[:, D_FF:])
    act = (_silu(gate) * up).astype(BF16)
    o_ref[...] = x1 + _rmsnorm(_dot(act, wd_ref[...]), gfo_ref[...])


def _merge_ffn(x2d, y_ssd, y_attn, ga, gb, wso, wao, wo, gmp, gfp, wgu, wd, gfo, tm):
    m = x2d.shape[0]
    row = lambda n: pl.BlockSpec((tm, n), lambda i: (i, 0))
    return pl.pallas_call(
        _merge_ffn_body,
        grid=(m // tm,),
        in_specs=[row(D_MODEL), row(D_INNER), row(ATTN_WIDTH), row(D_MODEL), row(D_MODEL),
                  _const_spec((D_INNER, D_MODEL)), _const_spec((ATTN_WIDTH, D_MODEL)),
                  _const_spec((D_MODEL, D_MODEL)), _const_spec((1, D_MODEL)), _const_spec((1, D_MODEL)),
                  _const_spec((D_MODEL, 2 * D_FF)), _const_spec((D_FF, D_MODEL)), _const_spec((1, D_MODEL))],
        out_specs=row(D_MODEL),
        out_shape=jax.ShapeDtypeStruct((m, D_MODEL), F32),
        compiler_params=pltpu.CompilerParams(dimension_semantics=("arbitrary",), vmem_limit_bytes=VMEM_LIMIT),
        name="merge_ffn",
    )(x2d, y_ssd, y_attn, ga, gb, wso, wao, wo, gmp, gfp, wgu, wd, gfo)


def _ssd_sample_body(xbc_ref, z_ref, sm_ref, sconv_ref, cw_ref, cbias_ref, dtb_ref, a_ref, dskip_ref,
                     gssd_ref, st_ref, y_ref, conv_ref, sto_ref,
                     xs_scr, bs_scr, cs_scr, xt_scr, dtt_scr, dect_scr, yt_scr):
    b = pl.program_id(0)
    nb = pl.num_programs(0)
    hd = SSD_HEAD_DIM
    gs = SSD_GROUPS * SSD_STATE

    @pl.when(b == 0)
    def _():
        xb = xbc_ref[...]
        conv = cbias_ref[...]
        for j in range(CONV_W - 1):
            conv = conv + cw_ref[j:j + 1, :] * sconv_ref[j]
        conv = conv + cw_ref[CONV_W - 1:CONV_W, :] * xb
        for j in range(CONV_W - 2):
            conv_ref[j] = sconv_ref[j + 1]
        conv_ref[CONV_W - 2] = xb
        act = _silu(conv)
        xs = act[:, :D_INNER]
        xs_scr[...] = xs
        bs_scr[...] = act[:, D_INNER:D_INNER + gs]
        cs_scr[...] = act[:, D_INNER + gs:]
        dt = _softplus(sm_ref[...] + dtb_ref[...])
        dec = jnp.exp(dt * -jnp.exp(a_ref[...]))
        dt_full = jnp.concatenate(
            [jnp.broadcast_to(dt[:, h:h + 1], (dt.shape[0], hd)) for h in range(SSD_HEADS)], axis=1)
        dec_full = jnp.concatenate(
            [jnp.broadcast_to(dec[:, h:h + 1], (dt.shape[0], hd)) for h in range(SSD_HEADS)], axis=1)
        xt_scr[...] = xs.T
        dtt_scr[...] = dt_full.T
        dect_scr[...] = dec_full.T
        yt_scr[...] = jnp.zeros_like(yt_scr)

    nbl = xt_scr.shape[1]
    lane_b = lax.broadcasted_iota(I32, (D_INNER, nbl), 1) == b

    def pick(ref):
        return jnp.sum(jnp.where(lane_b, ref[...], 0.0), axis=1, keepdims=True)

    x_col = pick(xt_scr)
    dt_col = pick(dtt_scr)
    dec_col = pick(dect_scr)
    b_row = bs_scr[pl.ds(b, 1), :]
    c_row = cs_scr[pl.ds(b, 1), :]
    rows_pg = (SSD_HEADS // SSD_GROUPS) * hd
    y_cols = []
    for g in range(SSD_GROUPS):
        r0 = g * rows_pg
        hst = st_ref[0, r0:r0 + rows_pg, :]
        bg = b_row[:, g * SSD_STATE:(g + 1) * SSD_STATE]
        cg = c_row[:, g * SSD_STATE:(g + 1) * SSD_STATE]
        hn = dec_col[r0:r0 + rows_pg] * hst + (x_col[r0:r0 + rows_pg] * bg) * dt_col[r0:r0 + rows_pg]
        sto_ref[0, r0:r0 + rows_pg, :] = hn
        y_cols.append(jnp.sum(hn * cg, axis=1, keepdims=True))
    y_col = jnp.concatenate(y_cols, axis=0)
    yt_scr[...] = jnp.where(lane_b, y_col, yt_scr[...])

    @pl.when(b == nb - 1)
    def _():
        y = yt_scr[...].T + dskip_ref[...] * xs_scr[...]
        u = y * _silu(z_ref[...])
        gw = D_INNER // SSD_GROUPS
        for g in range(SSD_GROUPS):
            ug = u[:, g * gw:(g + 1) * gw]
            y_ref[:, g * gw:(g + 1) * gw] = _rmsnorm(ug, gssd_ref[:, g * gw:(g + 1) * gw])


def _ssd_sample(xbc, z, small, sconv, conv_w, conv_b, dtb_pad, a_pad, dskip_full, g_ssd, state):
    nb = xbc.shape[0]
    full = lambda shape: pl.BlockSpec(shape, lambda b: (0,) * len(shape))
    rows = SSD_HEADS * SSD_HEAD_DIM
    st_spec = pl.BlockSpec((1, rows, SSD_STATE), lambda b: (b, 0, 0))
    return pl.pallas_call(
        _ssd_sample_body,
        grid=(nb,),
        in_specs=[full((nb, CONV_DIM)), full((nb, D_INNER)), full((nb, LANES)),
                  full((CONV_W - 1, nb, CONV_DIM)),
                  full((CONV_W, CONV_DIM)), full((1, CONV_DIM)), full((1, LANES)), full((1, LANES)),
                  full((1, D_INNER)), full((1, D_INNER)), st_spec],
        out_specs=[full((nb, D_INNER)), full((CONV_W - 1, nb, CONV_DIM)), st_spec],
        out_shape=[jax.ShapeDtypeStruct((nb, D_INNER), F32),
                   jax.ShapeDtypeStruct((CONV_W - 1, nb, CONV_DIM), F32),
                   jax.ShapeDtypeStruct((nb, rows, SSD_STATE), F32)],
        scratch_shapes=[pltpu.VMEM((nb, D_INNER), F32),
                        pltpu.VMEM((nb, SSD_GROUPS * SSD_STATE), F32),
                        pltpu.VMEM((nb, SSD_GROUPS * SSD_STATE), F32),
                        pltpu.VMEM((D_INNER, nb), F32), pltpu.VMEM((D_INNER, nb), F32),
                        pltpu.VMEM((D_INNER, nb), F32), pltpu.VMEM((D_INNER, nb), F32)],
        compiler_params=pltpu.CompilerParams(dimension_semantics=("arbitrary",), vmem_limit_bytes=VMEM_LIMIT),
        name="ssd_sample",
    )(xbc, z, small, sconv, conv_w, conv_b, dtb_pad, a_pad, dskip_full, g_ssd, state)


class _PagePipeline:
    def __init__(self, pt_ref, pairs, sem, pages, steps_per_seq):
        self.pt_ref, self.pairs, self.sem, self.pages, self.steps_per_seq = pt_ref, pairs, sem, pages, steps_per_seq

    def _copies(self, step, j):
        seq = step // self.steps_per_seq
        chunk = step % self.steps_per_seq
        slot = step % 2
        page = self.pt_ref[seq, chunk * self.pages + j]
        return [pltpu.make_async_copy(src.at[page], buf.at[slot, j], self.sem.at[slot]) for src, buf in self.pairs]

    def start(self, step):
        def body(j, carry):
            for cp in self._copies(step, j):
                cp.start()
            return carry
        lax.fori_loop(0, self.pages, body, 0)

    def wait(self, step):
        def body(j, carry):
            for cp in self._copies(step, j):
                cp.wait()
            return carry
        lax.fori_loop(0, self.pages, body, 0)

    def advance(self, step, n_steps):
        @pl.when(step == 0)
        def _():
            self.start(step)

        @pl.when(step + 1 < n_steps)
        def _():
            self.start(step + 1)

        self.wait(step)


def _idx_scores_body(pt_ref, qi_ref, wi_ref, cache_hbm, o_ref, buf, sem, *, pages):
    step = pl.program_id(0) * pl.num_programs(1) + pl.program_id(1)
    n_steps = pl.num_programs(0) * pl.num_programs(1)
    _PagePipeline(pt_ref, [(cache_hbm, buf)], sem, pages, pl.num_programs(1)).advance(step, n_steps)
    slot = step % 2
    qi = qi_ref[0]
    wi = wi_ref[0]
    sub = SAMPLE_SUB_PAGES
    for s in range(pages // sub):
        keys_t = jnp.concatenate([buf[slot, s * sub + j] for j in range(sub)], axis=1).astype(BF16)
        d = _dot(qi, keys_t)
        o_ref[0, :, s * sub * PAGE_SIZE:(s + 1) * sub * PAGE_SIZE] = jnp.sum(
            jnp.maximum(d, 0.0) * wi, axis=0, keepdims=True)


def _idx_scores(page_table, qi3, wi3, cache_idx, pages):
    nb, n_pages = page_table.shape
    nch = n_pages // pages
    span = pages * PAGE_SIZE
    grid_spec = pltpu.PrefetchScalarGridSpec(
        num_scalar_prefetch=1,
        grid=(nb, nch),
        in_specs=[pl.BlockSpec((1, IDX_HEADS, IDX_DIM), lambda b, c, pt: (b, 0, 0)),
                  pl.BlockSpec((1, IDX_HEADS, 1), lambda b, c, pt: (b, 0, 0)),
                  pl.BlockSpec(memory_space=pl.ANY)],
        out_specs=pl.BlockSpec((1, 1, span), lambda b, c, pt: (b, 0, c)),
        scratch_shapes=[pltpu.VMEM((2, pages, IDX_DIM, PAGE_SIZE), F32), pltpu.SemaphoreType.DMA((2,))],
    )
    return pl.pallas_call(
        functools.partial(_idx_scores_body, pages=pages),
        grid_spec=grid_spec,
        out_shape=jax.ShapeDtypeStruct((nb, 1, n_pages * PAGE_SIZE), F32),
        compiler_params=pltpu.CompilerParams(dimension_semantics=("arbitrary", "arbitrary"),
                                             vmem_limit_bytes=VMEM_LIMIT),
        name="idx_scores_sample",
    )(page_table, qi3, wi3, cache_idx)


def _select_sample_body(sc_ref, qi_ref, ki_ref, sm_ref, bias_ref, biasn_ref, key_scr, *, topk):
    rows, length = sc_ref.shape
    nkb = length // KEY_BLOCK
    for kb in range(nkb):
        key_scr[kb] = _sort_key(sc_ref[:, kb * KEY_BLOCK:(kb + 1) * KEY_BLOCK] + 0.0)
    ki = ki_ref[...].astype(F32)
    sm = sm_ref[...]
    sc_new = jnp.zeros((rows, 1), F32)
    for h in range(IDX_HEADS):
        d = jnp.sum(qi_ref[:, h * IDX_DIM:(h + 1) * IDX_DIM].astype(F32) * ki, axis=1, keepdims=True)
        sc_new = sc_new + jnp.maximum(d, 0.0) * sm[:, SM_WI + h:SM_WI + h + 1]
    key_new = _sort_key(sc_new + 0.0)
    thr = _kth_largest_key(key_scr, nkb, topk, rows, extra_key=key_new)
    for kb in range(nkb):
        bias_ref[:, kb * KEY_BLOCK:(kb + 1) * KEY_BLOCK] = jnp.where(key_scr[kb] >= thr, 0.0, NEG_BIG).astype(F32)
    biasn_ref[...] = jnp.broadcast_to(jnp.where(key_new >= thr, 0.0, NEG_BIG).astype(F32), biasn_ref.shape)


def _select_sample(scores, qib, kib, small, topk):
    nb, length = scores.shape
    return pl.pallas_call(
        functools.partial(_select_sample_body, topk=topk),
        out_shape=[jax.ShapeDtypeStruct((nb, length), F32), jax.ShapeDtypeStruct((nb, LANES), F32)],
        scratch_shapes=[pltpu.VMEM((length // KEY_BLOCK, nb, KEY_BLOCK), I32)],
        compiler_params=pltpu.CompilerParams(vmem_limit_bytes=VMEM_LIMIT),
        name="select_sample",
    )(scores, qib, kib, small)


def _attn_sample_body(pt_ref, q_ref, bias_ref, kn_ref, vn_ref, biasn_ref, ck_hbm, cv_hbm, o_ref,
                      kbuf, vbuf, sem, m_scr, l_scr, acc_scr, *, pages):
    c = pl.program_id(1)
    step = pl.program_id(0) * pl.num_programs(1) + c
    n_steps = pl.num_programs(0) * pl.num_programs(1)
    _PagePipeline(pt_ref, [(ck_hbm, kbuf), (cv_hbm, vbuf)], sem, pages, pl.num_programs(1)).advance(step, n_steps)
    slot = step % 2
    rep = N_HEADS // N_KV_HEADS

    @pl.when(c == 0)
    def _():
        m_scr[...] = jnp.full(m_scr.shape, NEG_BIG, F32)
        l_scr[...] = jnp.zeros_like(l_scr)
        acc_scr[...] = jnp.zeros_like(acc_scr)

    q = q_ref[0]
    head = lax.broadcasted_iota(I32, (N_HEADS, KV_WIDTH), 0)
    lane = lax.broadcasted_iota(I32, (N_HEADS, KV_WIDTH), 1)
    own = (lane // HEAD_DIM) == (head // rep)
    q_bd = jnp.where(own, jnp.concatenate([q.astype(F32)] * N_KV_HEADS, axis=1), 0.0).astype(BF16)
    sub = SAMPLE_SUB_PAGES
    for s in range(pages // sub):
        kk_t = jnp.concatenate([kbuf[slot, s * sub + j] for j in range(sub)], axis=1).astype(BF16)
        vv_t = jnp.concatenate([vbuf[slot, s * sub + j] for j in range(sub)], axis=1).astype(BF16)
        lg = _dot(q_bd, kk_t) + bias_ref[0, :, s * sub * PAGE_SIZE:(s + 1) * sub * PAGE_SIZE]
        m_old = m_scr[...]
        m_new = jnp.maximum(m_old, jnp.max(lg, axis=1, keepdims=True))
        alpha = jnp.exp(m_old - m_new)
        p = jnp.exp(lg - m_new)
        l_scr[...] = alpha * l_scr[...] + jnp.sum(p, axis=1, keepdims=True)
        acc_scr[...] = alpha * acc_scr[...] + _dot_nt(p.astype(BF16), vv_t)
        m_scr[...] = m_new

    @pl.when(c == pl.num_programs(1) - 1)
    def _():
        kn = kn_ref[0].astype(F32)
        lg_n = jnp.sum(q_bd.astype(F32) * kn, axis=1, keepdims=True) + biasn_ref[0][:, 0:1]
        m_o = m_scr[...]
        m_n = jnp.maximum(m_o, lg_n)
        al = jnp.exp(m_o - m_n)
        pn = jnp.exp(lg_n - m_n)
        l_fin = al * l_scr[...] + pn
        acc = al * acc_scr[...] + pn.astype(BF16).astype(F32) * vn_ref[0].astype(F32)
        res = jnp.where(own, acc / l_fin, 0.0)
        out = res[:, 0:HEAD_DIM]
        for g in range(1, N_KV_HEADS):
            out = out + res[:, g * HEAD_DIM:(g + 1) * HEAD_DIM]
        o_ref[0] = out


def _attn_sample(page_table, q3, bias3, kn3, vn3, biasn3, cache_k, cache_v, pages):
    nb, n_pages = page_table.shape
    nch = n_pages // pages
    span = pages * PAGE_SIZE
    per_b = lambda s: pl.BlockSpec((1,) + s, lambda b, c, pt: (b, 0, 0))
    grid_spec = pltpu.PrefetchScalarGridSpec(
        num_scalar_prefetch=1,
        grid=(nb, nch),
        in_specs=[per_b((N_HEADS, HEAD_DIM)),
                  pl.BlockSpec((1, 1, span), lambda b, c, pt: (b, 0, c)),
                  per_b((1, KV_WIDTH)), per_b((1, KV_WIDTH)), per_b((1, LANES)),
                  pl.BlockSpec(memory_space=pl.ANY), pl.BlockSpec(memory_space=pl.ANY)],
        out_specs=per_b((N_HEADS, HEAD_DIM)),
        scratch_shapes=[pltpu.VMEM((2, pages, KV_WIDTH, PAGE_SIZE), F32),
                        pltpu.VMEM((2, pages, KV_WIDTH, PAGE_SIZE), F32),
                        pltpu.SemaphoreType.DMA((2,)),
                        pltpu.VMEM((N_HEADS, 1), F32), pltpu.VMEM((N_HEADS, 1), F32),
                        pltpu.VMEM((N_HEADS, KV_WIDTH), F32)],
    )
    return pl.pallas_call(
        functools.partial(_attn_sample_body, pages=pages),
        grid_spec=grid_spec,
        out_shape=jax.ShapeDtypeStruct((nb, N_HEADS, HEAD_DIM), F32),
        compiler_params=pltpu.CompilerParams(dimension_semantics=("arbitrary", "arbitrary"),
                                             vmem_limit_bytes=VMEM_LIMIT),
        name="attn_sample",
    )(page_table, q3, bias3, kn3, vn3, biasn3, cache_k, cache_v)


def _rope_tables(pos):
    half = HEAD_DIM // 2
    inv = ROPE_THETA ** (-(jnp.arange(half, dtype=F32) * 2.0) / HEAD_DIM)
    ang = pos.astype(F32)[:, None] * inv[None, :]
    cos = jnp.cos(ang)
    sin = jnp.sin(ang)
    cos_t = jnp.concatenate([cos, cos, cos, cos], axis=1)
    sin_t = jnp.concatenate([-sin, sin, -sin, sin], axis=1)
    return cos_t, sin_t


def _permute_w_in(w):
    sizes = (D_INNER, CONV_DIM, SSD_HEADS, ATTN_WIDTH, KV_WIDTH, KV_WIDTH,
             IDX_HEADS * IDX_DIM, IDX_DIM, IDX_HEADS, D_MODEL, D_MODEL)
    cuts = np.concatenate([[0], np.cumsum(sizes)])
    z, xbc, dt, q, k, v, qi, ki, wi, ga, gb = [w[:, int(cuts[j]):int(cuts[j + 1])] for j in range(len(sizes))]
    pad = jnp.zeros((w.shape[0], SM_KI - SM_WI - IDX_HEADS), w.dtype)
    return jnp.concatenate([z, xbc, q, k, v, qi, ga, gb, dt, wi, pad, ki], axis=1).astype(BF16)


def _pad_lanes(v, offset=0):
    out = jnp.zeros((1, LANES), F32)
    return out.at[0, offset:offset + v.shape[0]].set(v.astype(F32))


def kernel(x_prompt, x_sample, cache_k, cache_v, cache_idx_k, state_ssm, state_conv, page_table, g_mix_pre, w_in,
           g_idx_k, b_idx_k, conv_w, conv_b, dt_bias, a_log, d_skip, g_ssd, w_ssd_out, w_attn_out, w_o, g_mix_post,
           g_ffn_pre, w_gate_up, w_down, g_ffn_post):
    bp, sp, _ = x_prompt.shape
    bd, ts, _ = x_sample.shape
    assert ts == 1 and w_in.shape[0] == 1, "one decode token per sample sequence, depth 1"
    n_pages = page_table.shape[1]
    past = n_pages * PAGE_SIZE
    layer = 0

    w_perm = _permute_w_in(w_in[layer])
    g_pre = g_mix_pre[layer][None, :]
    gik = _pad_lanes(g_idx_k[layer], SM_KI)
    bik = _pad_lanes(b_idx_k[layer], SM_KI)
    dtb = _pad_lanes(dt_bias[layer])
    a_pad = _pad_lanes(a_log[layer])
    dskip_full = jnp.repeat(d_skip[layer].astype(F32), SSD_HEAD_DIM)[None, :]
    gssd = g_ssd[layer][None, :]
    cw = conv_w[layer]
    cbias = conv_b[layer][None, :]
    wso = w_ssd_out[layer].astype(BF16)
    wao = w_attn_out[layer].astype(BF16)
    wo = w_o[layer].astype(BF16)
    wgu = w_gate_up[layer].astype(BF16)
    wd = w_down[layer].astype(BF16)
    gmp = g_mix_post[layer][None, :]
    gfp = g_ffn_pre[layer][None, :]
    gfo = g_ffn_post[layer][None, :]

    tm = 256
    xp2 = x_prompt.reshape(bp * sp, D_MODEL)
    cos_p, sin_p = _rope_tables(jnp.arange(sp))
    tiles_per_seq = sp // tm
    (z, xbc, k, v, ga, gb, ki, small, qb16, kb16, vb16, qib, kib) = _in_projection(
        xp2, cos_p, sin_p, lambda i: (i % tiles_per_seq, 0), g_pre, w_perm, gik, bik, tm)
    y_ssd, ssm_p, tail = _ssd_prompt(xbc, z, small, cw, cbias, dtb, a_pad, dskip_full, gssd, bp, sp)
    wi_t = small[:, SM_WI:SM_WI + IDX_HEADS].T
    y_attn = _dsa_prompt(qib, wi_t, qb16, kib, kb16, _values_feature_major(vb16, PROMPT_KEY_BLOCK), bp, sp)
    yp = _merge_ffn(xp2, y_ssd, y_attn, ga, gb, wso, wao, wo, gmp, gfp, wgu, wd, gfo, tm)

    y_prompt = yp.reshape(bp, sp, D_MODEL)
    k_prompt = k.reshape(1, bp, sp, N_KV_HEADS, HEAD_DIM)
    v_prompt = v.reshape(1, bp, sp, N_KV_HEADS, HEAD_DIM)
    idx_k_prompt = ki.reshape(1, bp, sp, IDX_DIM)
    ssm_prompt = ssm_p[None]
    conv_prompt = tail[None, :, SUBLANES - (CONV_W - 1):, :]

    xs2 = x_sample.reshape(bd, D_MODEL)
    cos_s, sin_s = _rope_tables(jnp.full((bd,), past, jnp.int32))
    (z, xbc, k, v, ga, gb, ki, small, qb16, kb16, vb16, qib, kib) = _in_projection(
        xs2, cos_s, sin_s, lambda i: (i, 0), g_pre, w_perm, gik, bik, bd)
    sconv = jnp.transpose(state_conv[layer], (1, 0, 2))
    st_in = state_ssm[layer].reshape(bd, SSD_HEADS * SSD_HEAD_DIM, SSD_STATE)
    y_ssd, conv_s, st_out = _ssd_sample(xbc, z, small, sconv, cw, cbias, dtb, a_pad, dskip_full, gssd, st_in)

    assert n_pages % SAMPLE_SUB_PAGES == 0, "page count must be a multiple of the per-matmul page group"
    idx_pages = math.gcd(IDX_PAGES_PER_STEP, n_pages)
    attn_pages = math.gcd(ATTN_PAGES_PER_STEP, n_pages)
    topk = min(TOPK_MAX, (past + ts) // 4)
    wi3 = small[:, SM_WI:SM_WI + IDX_HEADS].reshape(bd, IDX_HEADS, 1)
    cidx = jnp.transpose(cache_idx_k[layer], (0, 2, 1))
    ck = jnp.transpose(cache_k[layer], (0, 2, 3, 1)).reshape(-1, KV_WIDTH, PAGE_SIZE)
    cv = jnp.transpose(cache_v[layer], (0, 2, 3, 1)).reshape(-1, KV_WIDTH, PAGE_SIZE)
    scores = _idx_scores(page_table, qib.reshape(bd, IDX_HEADS, IDX_DIM), wi3, cidx, idx_pages)
    bias, bias_new = _select_sample(scores.reshape(bd, past), qib, kib, small, topk)
    y_attn = _attn_sample(page_table, qb16.reshape(bd, N_HEADS, HEAD_DIM), bias.reshape(bd, 1, past),
                          kb16.reshape(bd, 1, KV_WIDTH), vb16.reshape(bd, 1, KV_WIDTH),
                          bias_new.reshape(bd, 1, LANES), ck, cv, attn_pages)
    ys = _merge_ffn(xs2, y_ssd, y_attn.reshape(bd, ATTN_WIDTH), ga, gb, wso, wao, wo, gmp, gfp, wgu, wd, gfo, bd)

    y_sample = ys.reshape(bd, ts, D_MODEL)
    k_sample = k.reshape(1, bd, ts, N_KV_HEADS, HEAD_DIM)
    v_sample = v.reshape(1, bd, ts, N_KV_HEADS, HEAD_DIM)
    idx_k_sample = ki.reshape(1, bd, ts, IDX_DIM)
    ssm_sample = st_out.reshape(1, bd, SSD_HEADS, SSD_HEAD_DIM, SSD_STATE)
    conv_sample = jnp.transpose(conv_s, (1, 0, 2))[None]
    return (y_prompt, y_sample, k_prompt, v_prompt, idx_k_prompt, ssm_prompt, conv_prompt,
            k_sample, v_sample, idx_k_sample, ssm_sample, conv_sample)
```

```python
import functools
import math

import numpy as np
import jax
import jax.numpy as jnp
from jax import lax
from jax.experimental import pallas as pl
from jax.experimental.pallas import tpu as pltpu

F32 = jnp.float32
BF16 = jnp.bfloat16
I32 = jnp.int32

D_MODEL = 1024
D_INNER = 2048
SSD_HEAD_DIM = 64
SSD_HEADS = 32
SSD_GROUPS = 4
SSD_STATE = 128
CONV_W = 4
CONV_DIM = D_INNER + 2 * SSD_GROUPS * SSD_STATE
SSD_CHUNK = 128
N_HEADS = 16
N_KV_HEADS = 4
HEAD_DIM = 64
ATTN_WIDTH = N_HEADS * HEAD_DIM
KV_WIDTH = N_KV_HEADS * HEAD_DIM
IDX_HEADS = 8
IDX_DIM = 64
IDX_SCALE = (IDX_HEADS ** -0.5) * (IDX_DIM ** -0.5)
TOPK_MAX = 256
ROPE_THETA = 10000.0
PAGE_SIZE = 128
D_FF = 2816
EPS = 1e-6

LANES = 128
SUBLANES = 8
VMEM_LIMIT = 60 * 1024 * 1024

Z0 = 0
XBC0 = Z0 + D_INNER
Q0 = XBC0 + CONV_DIM
K0 = Q0 + ATTN_WIDTH
V0 = K0 + KV_WIDTH
QI0 = V0 + KV_WIDTH
GA0 = QI0 + IDX_HEADS * IDX_DIM
GB0 = GA0 + D_MODEL
SM0 = GB0 + D_MODEL
N_PROJ = SM0 + LANES
SM_DT = 0
SM_WI = SSD_HEADS
SM_KI = 64

Q_SCALE_LOG2 = (HEAD_DIM ** -0.5) * math.log2(math.e)
NEG_BIG = -1e30
INT_MIN = -(2 ** 31)
KEY_NEG_INF = -2139095041
KEY_BLOCK = 512
PROMPT_Q_BLOCK = LANES
PROMPT_SCORE_BLOCK = 512
PROMPT_KEY_BLOCK = 512
V_ROWS = 80
IDX_PAGES_PER_STEP = 64
ATTN_PAGES_PER_STEP = 32
SAMPLE_SUB_PAGES = 16


def _dot(a, b):
    return jnp.dot(a, b, preferred_element_type=F32)


def _dot_nt(a, b):
    return lax.dot_general(a, b, (((1,), (1,)), ((), ())), preferred_element_type=F32)


def _sigmoid(x):
    return 1.0 / (1.0 + jnp.exp(-x))


def _silu(x):
    return x * _sigmoid(x)


def _softplus(x):
    return jnp.maximum(x, 0.0) + jnp.log1p(jnp.exp(-jnp.abs(x)))


def _rmsnorm(x, g):
    return x * lax.rsqrt(jnp.mean(x * x, axis=-1, keepdims=True) + EPS) * g


def _rope_tile(x, cos, sin_signed, first_half):
    partner = jnp.where(first_half, pltpu.roll(x, LANES - 32, 1), pltpu.roll(x, 32, 1))
    return x * cos + partner * sin_signed


def _rope_wide(x, cos, sin_signed, first_half):
    n = x.shape[1] // LANES
    return jnp.concatenate(
        [_rope_tile(x[:, c * LANES:(c + 1) * LANES], cos, sin_signed, first_half) for c in range(n)], axis=1)


def _sort_key(x):
    bits = pltpu.bitcast(x, I32)
    return bits ^ ((bits >> 31) & 0x7FFFFFFF)


def _const_spec(shape):
    nd = len(shape)
    return pl.BlockSpec(shape, lambda *_: (0,) * nd, pipeline_mode=pl.Buffered(1))


def _inproj_body(x_ref, cos_ref, sin_ref, g_ref, w_ref, gik_ref, bik_ref,
                 z_ref, xbc_ref, k_ref, v_ref, ga_ref, gb_ref, ki_ref, sm_ref,
                 qb_ref, kb_ref, vb_ref, qib_ref, kib_ref):
    h = _rmsnorm(x_ref[...], g_ref[...]).astype(BF16)
    cos = cos_ref[...]
    sin = sin_ref[...]
    lane = lax.broadcasted_iota(I32, cos.shape, 1)
    first_half = (lane % HEAD_DIM) < (HEAD_DIM // 2)

    def seg(off, n):
        return _dot(h, w_ref[:, off:off + n])

    z_ref[...] = seg(Z0, D_INNER)
    xbc_ref[...] = seg(XBC0, CONV_DIM)
    ga_ref[...] = seg(GA0, D_MODEL)
    gb_ref[...] = seg(GB0, D_MODEL)
    q = _rope_wide(seg(Q0, ATTN_WIDTH), cos, sin, first_half)
    qb_ref[...] = (q * Q_SCALE_LOG2).astype(BF16)
    k = _rope_wide(seg(K0, KV_WIDTH), cos, sin, first_half)
    k_ref[...] = k
    kb_ref[...] = k.astype(BF16)
    v = seg(V0, KV_WIDTH)
    v_ref[...] = v
    vb_ref[...] = v.astype(BF16)
    qib_ref[...] = _rope_wide(seg(QI0, IDX_HEADS * IDX_DIM), cos, sin, first_half).astype(BF16)
    small = seg(SM0, LANES)
    is_ki = lane >= SM_KI
    mu = jnp.sum(jnp.where(is_ki, small, 0.0), axis=-1, keepdims=True) * (1.0 / IDX_DIM)
    dev = jnp.where(is_ki, small - mu, 0.0)
    var = jnp.sum(dev * dev, axis=-1, keepdims=True) * (1.0 / IDX_DIM)
    kin = dev * lax.rsqrt(var + EPS) * gik_ref[...] + bik_ref[...]
    kir = _rope_tile(kin, cos, sin, first_half)
    ki_ref[...] = kir[:, SM_KI:]
    kib_ref[...] = kir[:, SM_KI:].astype(BF16)
    is_wi = (lane >= SM_WI) & (lane < SM_WI + IDX_HEADS)
    sm_ref[...] = jnp.where(is_wi, small * IDX_SCALE, small)


def _in_projection(x2d, cos_tab, sin_tab, tab_index, g, w_perm, gik, bik, tm):
    m = x2d.shape[0]
    row = lambda n: pl.BlockSpec((tm, n), lambda i: (i, 0))
    out_shapes = [
        jax.ShapeDtypeStruct((m, D_INNER), F32),
        jax.ShapeDtypeStruct((m, CONV_DIM), F32),
        jax.ShapeDtypeStruct((m, KV_WIDTH), F32),
        jax.ShapeDtypeStruct((m, KV_WIDTH), F32),
        jax.ShapeDtypeStruct((m, D_MODEL), F32),
        jax.ShapeDtypeStruct((m, D_MODEL), F32),
        jax.ShapeDtypeStruct((m, IDX_DIM), F32),
        jax.ShapeDtypeStruct((m, LANES), F32),
        jax.ShapeDtypeStruct((m, ATTN_WIDTH), BF16),
        jax.ShapeDtypeStruct((m, KV_WIDTH), BF16),
        jax.ShapeDtypeStruct((m, KV_WIDTH), BF16),
        jax.ShapeDtypeStruct((m, IDX_HEADS * IDX_DIM), BF16),
        jax.ShapeDtypeStruct((m, IDX_DIM), BF16),
    ]
    return pl.pallas_call(
        _inproj_body,
        grid=(m // tm,),
        in_specs=[
            row(D_MODEL),
            pl.BlockSpec((tm, LANES), tab_index),
            pl.BlockSpec((tm, LANES), tab_index),
            _const_spec((1, D_MODEL)),
            _const_spec((D_MODEL, N_PROJ)),
            _const_spec((1, LANES)),
            _const_spec((1, LANES)),
        ],
        out_specs=[row(s.shape[1]) for s in out_shapes],
        out_shape=out_shapes,
        compiler_params=pltpu.CompilerParams(dimension_semantics=("arbitrary",), vmem_limit_bytes=VMEM_LIMIT),
        name="in_projection",
    )(x2d, cos_tab, sin_tab, g, w_perm, gik, bik)


def _pair_cols(col_a, col_b, first):
    return jnp.where(first, col_a, col_b)


def _ssd_prompt_body(xbc_ref, z_ref, sm_ref, cw_ref, cbias_ref, dtb_ref, a_ref, dskip_ref, gssd_ref,
                     y_ref, ssm_ref, tail_ref, xp_scr, st_scr, y_scr):
    c = pl.program_id(1)
    q = SSD_CHUNK

    @pl.when(c == 0)
    def _():
        xp_scr[0:SUBLANES, :] = jnp.zeros((SUBLANES, CONV_DIM), F32)
        st_scr[...] = jnp.zeros_like(st_scr)

    xb = xbc_ref[...]
    xp_scr[SUBLANES:SUBLANES + q, :] = xb
    conv = cbias_ref[...]
    for j in range(CONV_W - 1):
        lo = SUBLANES - (CONV_W - 1) + j
        conv = conv + cw_ref[j:j + 1, :] * xp_scr[lo:lo + q, :]
    conv = conv + cw_ref[CONV_W - 1:CONV_W, :] * xb
    xp_scr[SUBLANES - (CONV_W - 1):SUBLANES, :] = xb[q - (CONV_W - 1):q, :]
    tail_ref[0] = xb[q - SUBLANES:q, :]

    act = _silu(conv)
    xs = act[:, :D_INNER]
    gs = SSD_GROUPS * SSD_STATE
    bs = act[:, D_INNER:D_INNER + gs]
    cs = act[:, D_INNER + gs:]

    lane = lax.broadcasted_iota(I32, (q, LANES), 1)
    rowi = lax.broadcasted_iota(I32, (q, LANES), 0)
    causal = rowi >= lane
    first = lane < SSD_HEAD_DIM
    dt = jnp.where(lane < SSD_HEADS, _softplus(sm_ref[...] + dtb_ref[...]), 0.0)
    dta = dt * -jnp.exp(a_ref[...])
    tril = jnp.where(causal, 1.0, 0.0).astype(F32)
    cum = jnp.dot(tril, dta, preferred_element_type=F32, precision=lax.Precision.HIGHEST)
    cum_t = cum.T
    cum_last = cum[q - 1:q, :]
    e_cum = jnp.exp(cum)
    d_last = jnp.exp(cum_last - cum)
    e_last = jnp.exp(cum_last)

    rpg = SSD_HEADS // SSD_GROUPS
    for g in range(SSD_GROUPS):
        cs_g = cs[:, g * SSD_STATE:(g + 1) * SSD_STATE].astype(BF16)
        bs_g = bs[:, g * SSD_STATE:(g + 1) * SSD_STATE]
        cb = _dot_nt(cs_g, bs_g.astype(BF16))
        bs_t = bs_g.T.astype(BF16)
        for pr in range(rpg // 2):
            h0 = g * rpg + 2 * pr
            h1 = h0 + 1
            hp = h0 // 2
            xs_p = xs[:, hp * LANES:(hp + 1) * LANES]
            xdt = xs_p * _pair_cols(dt[:, h0:h0 + 1], dt[:, h1:h1 + 1], first)
            xdt_b = xdt.astype(BF16)
            yd = []
            for h in (h0, h1):
                seg = jnp.where(causal, cum[:, h:h + 1] - cum_t[h:h + 1, :], -jnp.inf)
                m = (cb * jnp.exp(seg)).astype(BF16)
                yd.append(_dot(m, xdt_b))
            st = st_scr[hp]
            y_off = _dot(cs_g, st.astype(BF16)) * _pair_cols(e_cum[:, h0:h0 + 1], e_cum[:, h1:h1 + 1], first)
            y_scr[:, hp * LANES:(hp + 1) * LANES] = jnp.where(first, yd[0], yd[1]) + y_off
            xdl = xdt * _pair_cols(d_last[:, h0:h0 + 1], d_last[:, h1:h1 + 1], first)
            dec = _pair_cols(e_last[:, h0:h0 + 1], e_last[:, h1:h1 + 1], first[0:1, :])
            st_scr[hp] = dec * st + _dot(bs_t, xdl.astype(BF16))

    y = y_scr[...] + dskip_ref[...] * xs
    u = y * _silu(z_ref[...])
    gw = D_INNER // SSD_GROUPS
    for g in range(SSD_GROUPS):
        ug = u[:, g * gw:(g + 1) * gw]
        y_ref[:, g * gw:(g + 1) * gw] = _rmsnorm(ug, gssd_ref[:, g * gw:(g + 1) * gw])

    @pl.when(c == pl.num_programs(1) - 1)
    def _():
        for hp in range(SSD_HEADS // 2):
            st_t = st_scr[hp].T
            ssm_ref[0, 2 * hp] = st_t[:SSD_HEAD_DIM, :]
            ssm_ref[0, 2 * hp + 1] = st_t[SSD_HEAD_DIM:, :]


def _ssd_prompt(xbc, z, small, conv_w, conv_b, dtb_pad, a_pad, dskip_full, g_ssd, bsz, seq):
    nc = seq // SSD_CHUNK
    q = SSD_CHUNK
    tok = lambda n: pl.BlockSpec((q, n), lambda b, c: (b * nc + c, 0))
    return pl.pallas_call(
        _ssd_prompt_body,
        grid=(bsz, nc),
        in_specs=[
            tok(CONV_DIM), tok(D_INNER), tok(LANES),
            _const_spec((CONV_W, CONV_DIM)), _const_spec((1, CONV_DIM)),
            _const_spec((1, LANES)), _const_spec((1, LANES)),
            _const_spec((1, D_INNER)), _const_spec((1, D_INNER)),
        ],
        out_specs=[
            tok(D_INNER),
            pl.BlockSpec((1, SSD_HEADS, SSD_HEAD_DIM, SSD_STATE), lambda b, c: (b, 0, 0, 0)),
            pl.BlockSpec((1, SUBLANES, CONV_DIM), lambda b, c: (b, 0, 0)),
        ],
        out_shape=[
            jax.ShapeDtypeStruct((bsz * seq, D_INNER), F32),
            jax.ShapeDtypeStruct((bsz, SSD_HEADS, SSD_HEAD_DIM, SSD_STATE), F32),
            jax.ShapeDtypeStruct((bsz, SUBLANES, CONV_DIM), F32),
        ],
        scratch_shapes=[
            pltpu.VMEM((SUBLANES + q, CONV_DIM), F32),
            pltpu.VMEM((SSD_HEADS // 2, SSD_STATE, LANES), F32),
            pltpu.VMEM((q, D_INNER), F32),
        ],
        compiler_params=pltpu.CompilerParams(dimension_semantics=("arbitrary", "arbitrary"),
                                             vmem_limit_bytes=VMEM_LIMIT),
        name="ssd_prompt",
    )(xbc, z, small, conv_w, conv_b, dtb_pad, a_pad, dskip_full, g_ssd)


def _kth_largest_rows(key_scr, nkb, k, rows, extra_key, tie_scr):
    n_total = nkb * KEY_BLOCK
    lane = lax.broadcasted_iota(I32, (rows, KEY_BLOCK), 1)

    def lane_fold(x):
        out = x[:, :LANES]
        for c in range(1, KEY_BLOCK // LANES):
            out = out + x[:, c * LANES:(c + 1) * LANES]
        return out

    def count(pred_block):
        def body(kb, cnt):
            return cnt + lane_fold(pred_block(kb).astype(I32))
        cnt = lax.fori_loop(0, nkb, body, jnp.zeros((rows, LANES), I32))
        return jnp.sum(cnt, axis=1, keepdims=True)

    def count_ge(cand):
        return count(lambda kb: key_scr[kb] >= cand) + (extra_key >= cand).astype(I32)

    zero = jnp.zeros((rows, 1), I32)
    t0 = jnp.where(count_ge(zero) >= k, zero, jnp.full((rows, 1), INT_MIN, I32))

    def bit_step(j, t):
        cand = t | jnp.left_shift(jnp.int32(1), 30 - j)
        return jnp.where(count_ge(cand) >= k, cand, t)

    thr = lax.fori_loop(0, 31, bit_step, t0)

    tie_scr[...] = jnp.full((rows, 1), n_total, I32)
    tied = (count_ge(thr) > k) & (thr > KEY_NEG_INF)

    @pl.when(jnp.max(tied.astype(I32)) > 0)
    def _():
        need = k - count_ge(thr + 1)

        def count_tied_below(limit):
            return count(lambda kb: (key_scr[kb] == thr) & (kb * KEY_BLOCK + lane < limit))

        n_bits = n_total.bit_length()

        def idx_step(j, lim):
            cand = lim | jnp.left_shift(jnp.int32(1), n_bits - 1 - j)
            return jnp.where(count_tied_below(cand) < need, cand, lim)

        last = lax.fori_loop(0, n_bits, idx_step, jnp.zeros((rows, 1), I32))
        tie_scr[...] = jnp.where(tied, last, n_total)

    return thr, tie_scr[...]


def _dsa_prompt_body(qi_ref, wit_ref, q_ref, ki_ref, k_ref, vt_ref, o_ref, key_scr, bias_scr, m_scr, acc_scr,
                     *, topk):
    i = pl.program_id(1)
    qb = PROMPT_Q_BLOCK
    sbs = PROMPT_SCORE_BLOCK
    kbs = PROMPT_KEY_BLOCK
    n_keys = i * qb + qb
    nsb = (n_keys + sbs - 1) // sbs
    nkb = (n_keys + kbs - 1) // kbs
    q_pos = i * qb + lax.broadcasted_iota(I32, (sbs, qb), 1)
    k_off = lax.broadcasted_iota(I32, (sbs, qb), 0)
    qi_stack = jnp.concatenate(
        [qi_ref[:, h * IDX_DIM:(h + 1) * IDX_DIM] for h in range(IDX_HEADS)], axis=0)
    w_rows = wit_ref[...]

    def score_block(sb, carry):
        start = pl.multiple_of(sb * sbs, sbs)
        d = _dot_nt(ki_ref[pl.ds(start, sbs), :], qi_stack)
        acc = jnp.zeros((sbs, qb), F32)
        for h in range(IDX_HEADS):
            acc = acc + jnp.maximum(d[:, h * qb:(h + 1) * qb], 0.0) * w_rows[h:h + 1, :]
        sc = jnp.where(start + k_off <= q_pos, acc + 0.0, -jnp.inf)
        key_scr[pl.ds(start, sbs), :] = _sort_key(sc)
        return carry

    lax.fori_loop(0, nsb, score_block, 0)

    lanes_acc = 4 * SUBLANES

    def count_ge(cand):
        def body(sb, acc):
            ge = (key_scr[pl.ds(pl.multiple_of(sb * sbs, sbs), sbs), :] >= cand).astype(I32)
            return acc + jnp.sum(ge.reshape(sbs // lanes_acc, lanes_acc, qb), axis=0)
        acc = lax.fori_loop(0, nsb, body, jnp.zeros((lanes_acc, qb), I32))
        return jnp.sum(acc, axis=0, keepdims=True)

    zero = jnp.zeros((1, qb), I32)
    t0 = jnp.where(count_ge(zero) >= topk, zero, jnp.full((1, qb), INT_MIN, I32))

    def bit_step(j, t):
        cand = t | jnp.left_shift(jnp.int32(1), 30 - j)
        return jnp.where(count_ge(cand) >= topk, cand, t)

    thr = lax.fori_loop(0, 31, bit_step, t0)

    n_total = key_scr.shape[0]
    tied = (count_ge(thr) > topk) & (thr > KEY_NEG_INF)
    any_tied = jnp.max(tied.astype(I32)) > 0

    def write_bias(select):
        def bias_block(sb, carry):
            start = pl.multiple_of(sb * sbs, sbs)
            idx = start + k_off
            sel = select(key_scr[pl.ds(start, sbs), :], idx) & (idx <= q_pos)
            bias_scr[pl.ds(start, sbs), :] = jnp.where(sel, 0.0, NEG_BIG).astype(F32)
            return carry
        lax.fori_loop(0, nsb, bias_block, 0)

    @pl.when(jnp.logical_not(any_tied))
    def _():
        write_bias(lambda key, idx: key >= thr)

    @pl.when(any_tied)
    def _():
        need = topk - count_ge(thr + 1)

        def count_tied_below(limit):
            def body(sb, acc):
                start = pl.multiple_of(sb * sbs, sbs)
                hit = ((key_scr[pl.ds(start, sbs), :] == thr) & (start + k_off < limit)).astype(I32)
                return acc + jnp.sum(hit.reshape(sbs // lanes_acc, lanes_acc, qb), axis=0)
            acc = lax.fori_loop(0, nsb, body, jnp.zeros((lanes_acc, qb), I32))
            return jnp.sum(acc, axis=0, keepdims=True)

        n_bits = max(1, (n_total - 1).bit_length())

        def idx_step(j, lim):
            cand = lim | jnp.left_shift(jnp.int32(1), n_bits - 1 - j)
            return jnp.where(count_tied_below(cand) < need, cand, lim)

        last = lax.fori_loop(0, n_bits, idx_step, jnp.zeros((1, qb), I32))
        tie_last = jnp.where(tied, last, n_total)
        write_bias(lambda key, idx: (key > thr) | ((key == thr) & (idx <= tie_last)))

    rep = N_HEADS // N_KV_HEADS
    m_scr[...] = jnp.full(m_scr.shape, NEG_BIG, m_scr.dtype)
    acc_scr[...] = jnp.zeros_like(acc_scr)

    groups = range(N_KV_HEADS)

    def masked_logits(kb):
        start = pl.multiple_of(kb * kbs, kbs)
        bias = jnp.concatenate([bias_scr[pl.ds(start, kbs), :]] * rep, axis=1)
        out = []
        for g in groups:
            qg = jnp.concatenate(
                [q_ref[:, (g * rep + r) * HEAD_DIM:(g * rep + r + 1) * HEAD_DIM] for r in range(rep)], axis=0)
            lg32 = _dot_nt(k_ref[pl.ds(start, kbs), g * HEAD_DIM:(g + 1) * HEAD_DIM], qg) + bias
            out.append(lg32.astype(BF16))
        return out

    def att_block(kb, carry):
        m_old = [m_scr[g] for g in groups]
        acc_old = [acc_scr[g] for g in groups]
        lg = masked_logits(kb)
        m_new = [jnp.maximum(m_old[g], jnp.max(lg[g], axis=0, keepdims=True)) for g in groups]
        p = [jnp.exp2(lg[g] - m_new[g]) for g in groups]
        pv = [_dot(vt_ref[kb, g], p[g]) for g in groups]
        for g in groups:
            alpha = jnp.exp2(m_old[g].astype(F32) - m_new[g].astype(F32))
            acc_scr[g] = alpha * acc_old[g] + pv[g]
            m_scr[g] = m_new[g]
        return carry

    lax.fori_loop(0, nkb, att_block, 0)

    for g in range(N_KV_HEADS):
        acc = acc_scr[g]
        out_t = acc[:HEAD_DIM, :] / acc[HEAD_DIM:HEAD_DIM + 1, :]
        for pr in range(rep // 2):
            pair = jnp.concatenate([out_t[:, (2 * pr) * qb:(2 * pr + 1) * qb],
                                    out_t[:, (2 * pr + 1) * qb:(2 * pr + 2) * qb]], axis=0)
            c0 = (g * rep + 2 * pr) * HEAD_DIM
            o_ref[:, c0:c0 + 2 * HEAD_DIM] = pair.T


def _dsa_prompt(qib, wi_t, qb16, kib, kb16, vt4, bsz, seq):
    topk = min(TOPK_MAX, seq // 4)
    qblk = PROMPT_Q_BLOCK
    kbs = PROMPT_KEY_BLOCK
    nq = seq // qblk
    tok = lambda n: pl.BlockSpec((qblk, n), lambda b, i: (b * nq + i, 0))
    per_seq = lambda n: pl.BlockSpec((seq, n), lambda b, i: (b, 0))
    return pl.pallas_call(
        functools.partial(_dsa_prompt_body, topk=topk),
        grid=(bsz, nq),
        in_specs=[tok(IDX_HEADS * IDX_DIM),
                  pl.BlockSpec((IDX_HEADS, qblk), lambda b, i: (0, b * nq + i)),
                  tok(ATTN_WIDTH),
                  per_seq(IDX_DIM), per_seq(KV_WIDTH),
                  pl.BlockSpec((seq // kbs, N_KV_HEADS, V_ROWS, kbs), lambda b, i: (b, 0, 0, 0))],
        out_specs=tok(ATTN_WIDTH),
        out_shape=jax.ShapeDtypeStruct((bsz * seq, ATTN_WIDTH), F32),
        scratch_shapes=[pltpu.VMEM((seq, qblk), I32), pltpu.VMEM((seq, qblk), F32),
                        pltpu.VMEM((N_KV_HEADS, 1, (N_HEADS // N_KV_HEADS) * qblk), BF16),
                        pltpu.VMEM((N_KV_HEADS, V_ROWS, (N_HEADS // N_KV_HEADS) * qblk), F32)],
        compiler_params=pltpu.CompilerParams(dimension_semantics=("arbitrary", "arbitrary"),
                                             vmem_limit_bytes=VMEM_LIMIT),
        name="dsa_prompt",
    )(qib, wi_t, qb16, kib, kb16, vt4)


def _values_feature_major(vb16, kbs):
    m = vb16.shape[0]
    v3 = vb16.reshape(m, N_KV_HEADS, HEAD_DIM)
    ones = jnp.ones((m, N_KV_HEADS, 1), BF16)
    pad = jnp.zeros((m, N_KV_HEADS, V_ROWS - HEAD_DIM - 1), BF16)
    v_aug = jnp.concatenate([v3, ones, pad], axis=2)
    return jnp.transpose(v_aug.reshape(m // kbs, kbs, N_KV_HEADS, V_ROWS), (0, 2, 3, 1))


def _merge_ffn_body(x_ref, ys_ref, ya_ref, ga_ref, gb_ref, wso_ref, wao_ref, wo_ref, gmp_ref, gfp_ref,
                    wgu_ref, wd_ref, gfo_ref, o_ref):
    mixed = (_sigmoid(ga_ref[...]) * _dot(ys_ref[...].astype(BF16), wso_ref[...])
             + _sigmoid(gb_ref[...]) * _dot(ya_ref[...].astype(BF16), wao_ref[...]))
    x1 = x_ref[...] + _rmsnorm(_dot(mixed.astype(BF16), wo_ref[...]), gmp_ref[...])
    h2 = _rmsnorm(x1, gfp_ref[...]).astype(BF16)
    gate = _dot(h2, wgu_ref[:, :D_FF])
    up = _dot(h2, wgu_ref[:, D_FF:])
    act = (_silu(gate) * up).astype(BF16)
    o_ref[...] = x1 + _rmsnorm(_dot(act, wd_ref[...]), gfo_ref[...])


def _merge_ffn(x2d, y_ssd, y_attn, ga, gb, wso, wao, wo, gmp, gfp, wgu, wd, gfo, tm):
    m = x2d.shape[0]
    row = lambda n: pl.BlockSpec((tm, n), lambda i: (i, 0))
    return pl.pallas_call(
        _merge_ffn_body,
        grid=(m // tm,),
        in_specs=[row(D_MODEL), row(D_INNER), row(ATTN_WIDTH), row(D_MODEL), row(D_MODEL),
                  _const_spec((D_INNER, D_MODEL)), _const_spec((ATTN_WIDTH, D_MODEL)),
                  _const_spec((D_MODEL, D_MODEL)), _const_spec((1, D_MODEL)), _const_spec((1, D_MODEL)),
                  _const_spec((D_MODEL, 2 * D_FF)), _const_spec((D_FF, D_MODEL)), _const_spec((1, D_MODEL))],
        out_specs=row(D_MODEL),
        out_shape=jax.ShapeDtypeStruct((m, D_MODEL), F32),
        compiler_params=pltpu.CompilerParams(dimension_semantics=("arbitrary",), vmem_limit_bytes=VMEM_LIMIT),
        name="merge_ffn",
    )(x2d, y_ssd, y_attn, ga, gb, wso, wao, wo, gmp, gfp, wgu, wd, gfo)


def _ssd_sample_body(xbc_ref, z_ref, sm_ref, sconv_ref, cw_ref, cbias_ref, dtb_ref, a_ref, dskip_ref,
                     gssd_ref, st_ref, y_ref, conv_ref, sto_ref,
                     xs_scr, bs_scr, cs_scr, xt_scr, dtt_scr, dect_scr, yt_scr):
    b = pl.program_id(0)
    nb = pl.num_programs(0)
    hd = SSD_HEAD_DIM
    gs = SSD_GROUPS * SSD_STATE

    @pl.when(b == 0)
    def _():
        xb = xbc_ref[...]
        conv = cbias_ref[...]
        for j in range(CONV_W - 1):
            conv = conv + cw_ref[j:j + 1, :] * sconv_ref[j]
        conv = conv + cw_ref[CONV_W - 1:CONV_W, :] * xb
        for j in range(CONV_W - 2):
            conv_ref[j] = sconv_ref[j + 1]
        conv_ref[CONV_W - 2] = xb
        act = _silu(conv)
        xs = act[:, :D_INNER]
        xs_scr[...] = xs
        bs_scr[...] = act[:, D_INNER:D_INNER + gs]
        cs_scr[...] = act[:, D_INNER + gs:]
        dt = _softplus(sm_ref[...] + dtb_ref[...])
        dec = jnp.exp(dt * -jnp.exp(a_ref[...]))
        dt_full = jnp.concatenate(
            [jnp.broadcast_to(dt[:, h:h + 1], (dt.shape[0], hd)) for h in range(SSD_HEADS)], axis=1)
        dec_full = jnp.concatenate(
            [jnp.broadcast_to(dec[:, h:h + 1], (dt.shape[0], hd)) for h in range(SSD_HEADS)], axis=1)
        xt_scr[...] = xs.T
        dtt_scr[...] = dt_full.T
        dect_scr[...] = dec_full.T
        yt_scr[...] = jnp.zeros_like(yt_scr)

    nbl = xt_scr.shape[1]
    lane_b = lax.broadcasted_iota(I32, (D_INNER, nbl), 1) == b

    def pick(ref):
        return jnp.sum(jnp.where(lane_b, ref[...], 0.0), axis=1, keepdims=True)

    x_col = pick(xt_scr)
    dt_col = pick(dtt_scr)
    dec_col = pick(dect_scr)
    b_row = bs_scr[pl.ds(b, 1), :]
    c_row = cs_scr[pl.ds(b, 1), :]
    rows_pg = (SSD_HEADS // SSD_GROUPS) * hd
    y_cols = []
    for g in range(SSD_GROUPS):
        r0 = g * rows_pg
        hst = st_ref[0, r0:r0 + rows_pg, :]
        bg = b_row[:, g * SSD_STATE:(g + 1) * SSD_STATE]
        cg = c_row[:, g * SSD_STATE:(g + 1) * SSD_STATE]
        hn = dec_col[r0:r0 + rows_pg] * hst + (x_col[r0:r0 + rows_pg] * bg) * dt_col[r0:r0 + rows_pg]
        sto_ref[0, r0:r0 + rows_pg, :] = hn
        y_cols.append(jnp.sum(hn * cg, axis=1, keepdims=True))
    y_col = jnp.concatenate(y_cols, axis=0)
    yt_scr[...] = jnp.where(lane_b, y_col, yt_scr[...])

    @pl.when(b == nb - 1)
    def _():
        y = yt_scr[...].T + dskip_ref[...] * xs_scr[...]
        u = y * _silu(z_ref[...])
        gw = D_INNER // SSD_GROUPS
        for g in range(SSD_GROUPS):
            ug = u[:, g * gw:(g + 1) * gw]
            y_ref[:, g * gw:(g + 1) * gw] = _rmsnorm(ug, gssd_ref[:, g * gw:(g + 1) * gw])


def _ssd_sample(xbc, z, small, sconv, conv_w, conv_b, dtb_pad, a_pad, dskip_full, g_ssd, state):
    nb = xbc.shape[0]
    full = lambda shape: pl.BlockSpec(shape, lambda b: (0,) * len(shape))
    rows = SSD_HEADS * SSD_HEAD_DIM
    st_spec = pl.BlockSpec((1, rows, SSD_STATE), lambda b: (b, 0, 0))
    return pl.pallas_call(
        _ssd_sample_body,
        grid=(nb,),
        in_specs=[full((nb, CONV_DIM)), full((nb, D_INNER)), full((nb, LANES)),
                  full((CONV_W - 1, nb, CONV_DIM)),
                  full((CONV_W, CONV_DIM)), full((1, CONV_DIM)), full((1, LANES)), full((1, LANES)),
                  full((1, D_INNER)), full((1, D_INNER)), st_spec],
        out_specs=[full((nb, D_INNER)), full((CONV_W - 1, nb, CONV_DIM)), st_spec],
        out_shape=[jax.ShapeDtypeStruct((nb, D_INNER), F32),
                   jax.ShapeDtypeStruct((CONV_W - 1, nb, CONV_DIM), F32),
                   jax.ShapeDtypeStruct((nb, rows, SSD_STATE), F32)],
        scratch_shapes=[pltpu.VMEM((nb, D_INNER), F32),
                        pltpu.VMEM((nb, SSD_GROUPS * SSD_STATE), F32),
                        pltpu.VMEM((nb, SSD_GROUPS * SSD_STATE), F32),
                        pltpu.VMEM((D_INNER, nb), F32), pltpu.VMEM((D_INNER, nb), F32),
                        pltpu.VMEM((D_INNER, nb), F32), pltpu.VMEM((D_INNER, nb), F32)],
        compiler_params=pltpu.CompilerParams(dimension_semantics=("arbitrary",), vmem_limit_bytes=VMEM_LIMIT),
        name="ssd_sample",
    )(xbc, z, small, sconv, conv_w, conv_b, dtb_pad, a_pad, dskip_full, g_ssd, state)


class _PagePipeline:
    def __init__(self, pt_ref, pairs, sem, pages, steps_per_seq):
        self.pt_ref, self.pairs, self.sem, self.pages, self.steps_per_seq = pt_ref, pairs, sem, pages, steps_per_seq

    def _copies(self, step, j):
        seq = step // self.steps_per_seq
        chunk = step % self.steps_per_seq
        slot = step % 2
        page = self.pt_ref[seq, chunk * self.pages + j]
        return [pltpu.make_async_copy(src.at[page], buf.at[slot, j], self.sem.at[slot]) for src, buf in self.pairs]

    def start(self, step):
        def body(j, carry):
            for cp in self._copies(step, j):
                cp.start()
            return carry
        lax.fori_loop(0, self.pages, body, 0)

    def wait(self, step):
        def body(j, carry):
            for cp in self._copies(step, j):
                cp.wait()
            return carry
        lax.fori_loop(0, self.pages, body, 0)

    def advance(self, step, n_steps):
        @pl.when(step == 0)
        def _():
            self.start(step)

        @pl.when(step + 1 < n_steps)
        def _():
            self.start(step + 1)

        self.wait(step)


def _idx_scores_body(pt_ref, qi_ref, wi_ref, cache_hbm, o_ref, buf, sem, *, pages):
    step = pl.program_id(0) * pl.num_programs(1) + pl.program_id(1)
    n_steps = pl.num_programs(0) * pl.num_programs(1)
    _PagePipeline(pt_ref, [(cache_hbm, buf)], sem, pages, pl.num_programs(1)).advance(step, n_steps)
    slot = step % 2
    qi = qi_ref[0]
    wi = wi_ref[0]
    sub = SAMPLE_SUB_PAGES
    for s in range(pages // sub):
        keys_t = jnp.concatenate([buf[slot, s * sub + j] for j in range(sub)], axis=1).astype(BF16)
        d = _dot(qi, keys_t)
        o_ref[0, :, s * sub * PAGE_SIZE:(s + 1) * sub * PAGE_SIZE] = jnp.sum(
            jnp.maximum(d, 0.0) * wi, axis=0, keepdims=True)


def _idx_scores(page_table, qi3, wi3, cache_idx, pages):
    nb, n_pages = page_table.shape
    nch = n_pages // pages
    span = pages * PAGE_SIZE
    grid_spec = pltpu.PrefetchScalarGridSpec(
        num_scalar_prefetch=1,
        grid=(nb, nch),
        in_specs=[pl.BlockSpec((1, IDX_HEADS, IDX_DIM), lambda b, c, pt: (b, 0, 0)),
                  pl.BlockSpec((1, IDX_HEADS, 1), lambda b, c, pt: (b, 0, 0)),
                  pl.BlockSpec(memory_space=pl.ANY)],
        out_specs=pl.BlockSpec((1, 1, span), lambda b, c, pt: (b, 0, c)),
        scratch_shapes=[pltpu.VMEM((2, pages, IDX_DIM, PAGE_SIZE), F32), pltpu.SemaphoreType.DMA((2,))],
    )
    return pl.pallas_call(
        functools.partial(_idx_scores_body, pages=pages),
        grid_spec=grid_spec,
        out_shape=jax.ShapeDtypeStruct((nb, 1, n_pages * PAGE_SIZE), F32),
        compiler_params=pltpu.CompilerParams(dimension_semantics=("arbitrary", "arbitrary"),
                                             vmem_limit_bytes=VMEM_LIMIT),
        name="idx_scores_sample",
    )(page_table, qi3, wi3, cache_idx)


def _select_sample_body(sc_ref, qi_ref, ki_ref, sm_ref, bias_ref, biasn_ref, key_scr, tie_scr, *, topk):
    rows, length = sc_ref.shape
    nkb = length // KEY_BLOCK
    for kb in range(nkb):
        key_scr[kb] = _sort_key(sc_ref[:, kb * KEY_BLOCK:(kb + 1) * KEY_BLOCK] + 0.0)
    ki = ki_ref[...].astype(F32)
    sm = sm_ref[...]
    sc_new = jnp.zeros((rows, 1), F32)
    for h in range(IDX_HEADS):
        d = jnp.sum(qi_ref[:, h * IDX_DIM:(h + 1) * IDX_DIM].astype(F32) * ki, axis=1, keepdims=True)
        sc_new = sc_new + jnp.maximum(d, 0.0) * sm[:, SM_WI + h:SM_WI + h + 1]
    key_new = _sort_key(sc_new + 0.0)
    thr, tie_last = _kth_largest_rows(key_scr, nkb, topk, rows, key_new, tie_scr)
    lane = lax.broadcasted_iota(I32, (rows, KEY_BLOCK), 1)
    for kb in range(nkb):
        key = key_scr[kb]
        sel = (key > thr) | ((key == thr) & (kb * KEY_BLOCK + lane <= tie_last))
        bias_ref[:, kb * KEY_BLOCK:(kb + 1) * KEY_BLOCK] = jnp.where(sel, 0.0, NEG_BIG).astype(F32)
    sel_new = (key_new > thr) | ((key_new == thr) & (length <= tie_last))
    biasn_ref[...] = jnp.broadcast_to(jnp.where(sel_new, 0.0, NEG_BIG).astype(F32), biasn_ref.shape)


def _select_sample(scores, qib, kib, small, topk):
    nb, length = scores.shape
    return pl.pallas_call(
        functools.partial(_select_sample_body, topk=topk),
        out_shape=[jax.ShapeDtypeStruct((nb, length), F32), jax.ShapeDtypeStruct((nb, LANES), F32)],
        scratch_shapes=[pltpu.VMEM((length // KEY_BLOCK, nb, KEY_BLOCK), I32), pltpu.VMEM((nb, 1), I32)],
        compiler_params=pltpu.CompilerParams(vmem_limit_bytes=VMEM_LIMIT),
        name="select_sample",
    )(scores, qib, kib, small)


def _attn_sample_body(pt_ref, q_ref, bias_ref, kn_ref, vn_ref, biasn_ref, ck_hbm, cv_hbm, o_ref,
                      kbuf, vbuf, sem, m_scr, l_scr, acc_scr, *, pages):
    c = pl.program_id(1)
    step = pl.program_id(0) * pl.num_programs(1) + c
    n_steps = pl.num_programs(0) * pl.num_programs(1)
    _PagePipeline(pt_ref, [(ck_hbm, kbuf), (cv_hbm, vbuf)], sem, pages, pl.num_programs(1)).advance(step, n_steps)
    slot = step % 2
    rep = N_HEADS // N_KV_HEADS

    @pl.when(c == 0)
    def _():
        m_scr[...] = jnp.full(m_scr.shape, NEG_BIG, F32)
        l_scr[...] = jnp.zeros_like(l_scr)
        acc_scr[...] = jnp.zeros_like(acc_scr)

    q = q_ref[0]
    head = lax.broadcasted_iota(I32, (N_HEADS, KV_WIDTH), 0)
    lane = lax.broadcasted_iota(I32, (N_HEADS, KV_WIDTH), 1)
    own = (lane // HEAD_DIM) == (head // rep)
    q_bd = jnp.where(own, jnp.concatenate([q.astype(F32)] * N_KV_HEADS, axis=1), 0.0).astype(BF16)
    sub = SAMPLE_SUB_PAGES
    for s in range(pages // sub):
        kk_t = jnp.concatenate([kbuf[slot, s * sub + j] for j in range(sub)], axis=1).astype(BF16)
        vv_t = jnp.concatenate([vbuf[slot, s * sub + j] for j in range(sub)], axis=1).astype(BF16)
        lg = _dot(q_bd, kk_t) + bias_ref[0, :, s * sub * PAGE_SIZE:(s + 1) * sub * PAGE_SIZE]
        m_old = m_scr[...]
        m_new = jnp.maximum(m_old, jnp.max(lg, axis=1, keepdims=True))
        alpha = jnp.exp2(m_old - m_new)
        p = jnp.exp2(lg - m_new)
        l_scr[...] = alpha * l_scr[...] + jnp.sum(p, axis=1, keepdims=True)
        acc_scr[...] = alpha * acc_scr[...] + _dot_nt(p.astype(BF16), vv_t)
        m_scr[...] = m_new

    @pl.when(c == pl.num_programs(1) - 1)
    def _():
        kn = kn_ref[0].astype(F32)
        lg_n = jnp.sum(q_bd.astype(F32) * kn, axis=1, keepdims=True) + biasn_ref[0][:, 0:1]
        m_o = m_scr[...]
        m_n = jnp.maximum(m_o, lg_n)
        al = jnp.exp2(m_o - m_n)
        pn = jnp.exp2(lg_n - m_n)
        l_fin = al * l_scr[...] + pn
        acc = al * acc_scr[...] + pn.astype(BF16).astype(F32) * vn_ref[0].astype(F32)
        res = jnp.where(own, acc / l_fin, 0.0)
        out = res[:, 0:HEAD_DIM]
        for g in range(1, N_KV_HEADS):
            out = out + res[:, g * HEAD_DIM:(g + 1) * HEAD_DIM]
        o_ref[0] = out


def _attn_sample(page_table, q3, bias3, kn3, vn3, biasn3, cache_k, cache_v, pages):
    nb, n_pages = page_table.shape
    nch = n_pages // pages
    span = pages * PAGE_SIZE
    per_b = lambda s: pl.BlockSpec((1,) + s, lambda b, c, pt: (b, 0, 0))
    grid_spec = pltpu.PrefetchScalarGridSpec(
        num_scalar_prefetch=1,
        grid=(nb, nch),
        in_specs=[per_b((N_HEADS, HEAD_DIM)),
                  pl.BlockSpec((1, 1, span), lambda b, c, pt: (b, 0, c)),
                  per_b((1, KV_WIDTH)), per_b((1, KV_WIDTH)), per_b((1, LANES)),
                  pl.BlockSpec(memory_space=pl.ANY), pl.BlockSpec(memory_space=pl.ANY)],
        out_specs=per_b((N_HEADS, HEAD_DIM)),
        scratch_shapes=[pltpu.VMEM((2, pages, KV_WIDTH, PAGE_SIZE), F32),
                        pltpu.VMEM((2, pages, KV_WIDTH, PAGE_SIZE), F32),
                        pltpu.SemaphoreType.DMA((2,)),
                        pltpu.VMEM((N_HEADS, 1), F32), pltpu.VMEM((N_HEADS, 1), F32),
                        pltpu.VMEM((N_HEADS, KV_WIDTH), F32)],
    )
    return pl.pallas_call(
        functools.partial(_attn_sample_body, pages=pages),
        grid_spec=grid_spec,
        out_shape=jax.ShapeDtypeStruct((nb, N_HEADS, HEAD_DIM), F32),
        compiler_params=pltpu.CompilerParams(dimension_semantics=("arbitrary", "arbitrary"),
                                             vmem_limit_bytes=VMEM_LIMIT),
        name="attn_sample",
    )(page_table, q3, bias3, kn3, vn3, biasn3, cache_k, cache_v)


def _rope_tables(pos):
    half = HEAD_DIM // 2
    inv = ROPE_THETA ** (-(jnp.arange(half, dtype=F32) * 2.0) / HEAD_DIM)
    ang = pos.astype(F32)[:, None] * inv[None, :]
    cos = jnp.cos(ang)
    sin = jnp.sin(ang)
    cos_t = jnp.concatenate([cos, cos, cos, cos], axis=1)
    sin_t = jnp.concatenate([-sin, sin, -sin, sin], axis=1)
    return cos_t, sin_t


def _permute_w_in(w):
    sizes = (D_INNER, CONV_DIM, SSD_HEADS, ATTN_WIDTH, KV_WIDTH, KV_WIDTH,
             IDX_HEADS * IDX_DIM, IDX_DIM, IDX_HEADS, D_MODEL, D_MODEL)
    cuts = np.concatenate([[0], np.cumsum(sizes)])
    z, xbc, dt, q, k, v, qi, ki, wi, ga, gb = [w[:, int(cuts[j]):int(cuts[j + 1])] for j in range(len(sizes))]
    pad = jnp.zeros((w.shape[0], SM_KI - SM_WI - IDX_HEADS), w.dtype)
    return jnp.concatenate([z, xbc, q, k, v, qi, ga, gb, dt, wi, pad, ki], axis=1).astype(BF16)


def _pad_lanes(v, offset=0):
    out = jnp.zeros((1, LANES), F32)
    return out.at[0, offset:offset + v.shape[0]].set(v.astype(F32))


def kernel(x_prompt, x_sample, cache_k, cache_v, cache_idx_k, state_ssm, state_conv, page_table, g_mix_pre, w_in,
           g_idx_k, b_idx_k, conv_w, conv_b, dt_bias, a_log, d_skip, g_ssd, w_ssd_out, w_attn_out, w_o, g_mix_post,
           g_ffn_pre, w_gate_up, w_down, g_ffn_post):
    bp, sp, _ = x_prompt.shape
    bd, ts, _ = x_sample.shape
    assert ts == 1 and w_in.shape[0] == 1, "one decode token per sample sequence, depth 1"
    n_pages = page_table.shape[1]
    past = n_pages * PAGE_SIZE
    layer = 0

    w_perm = _permute_w_in(w_in[layer])
    g_pre = g_mix_pre[layer][None, :]
    gik = _pad_lanes(g_idx_k[layer], SM_KI)
    bik = _pad_lanes(b_idx_k[layer], SM_KI)
    dtb = _pad_lanes(dt_bias[layer])
    a_pad = _pad_lanes(a_log[layer])
    dskip_full = jnp.repeat(d_skip[layer].astype(F32), SSD_HEAD_DIM)[None, :]
    gssd = g_ssd[layer][None, :]
    cw = conv_w[layer]
    cbias = conv_b[layer][None, :]
    wso = w_ssd_out[layer].astype(BF16)
    wao = w_attn_out[layer].astype(BF16)
    wo = w_o[layer].astype(BF16)
    wgu = w_gate_up[layer].astype(BF16)
    wd = w_down[layer].astype(BF16)
    gmp = g_mix_post[layer][None, :]
    gfp = g_ffn_pre[layer][None, :]
    gfo = g_ffn_post[layer][None, :]

    tm = 256
    xp2 = x_prompt.reshape(bp * sp, D_MODEL)
    cos_p, sin_p = _rope_tables(jnp.arange(sp))
    tiles_per_seq = sp // tm
    (z, xbc, k, v, ga, gb, ki, small, qb16, kb16, vb16, qib, kib) = _in_projection(
        xp2, cos_p, sin_p, lambda i: (i % tiles_per_seq, 0), g_pre, w_perm, gik, bik, tm)
    y_ssd, ssm_p, tail = _ssd_prompt(xbc, z, small, cw, cbias, dtb, a_pad, dskip_full, gssd, bp, sp)
    wi_t = small[:, SM_WI:SM_WI + IDX_HEADS].T
    y_attn = _dsa_prompt(qib, wi_t, qb16, kib, kb16, _values_feature_major(vb16, PROMPT_KEY_BLOCK), bp, sp)
    yp = _merge_ffn(xp2, y_ssd, y_attn, ga, gb, wso, wao, wo, gmp, gfp, wgu, wd, gfo, tm)

    y_prompt = yp.reshape(bp, sp, D_MODEL)
    k_prompt = k.reshape(1, bp, sp, N_KV_HEADS, HEAD_DIM)
    v_prompt = v.reshape(1, bp, sp, N_KV_HEADS, HEAD_DIM)
    idx_k_prompt = ki.reshape(1, bp, sp, IDX_DIM)
    ssm_prompt = ssm_p[None]
    conv_prompt = tail[None, :, SUBLANES - (CONV_W - 1):, :]

    xs2 = x_sample.reshape(bd, D_MODEL)
    cos_s, sin_s = _rope_tables(jnp.full((bd,), past, jnp.int32))
    (z, xbc, k, v, ga, gb, ki, small, qb16, kb16, vb16, qib, kib) = _in_projection(
        xs2, cos_s, sin_s, lambda i: (i, 0), g_pre, w_perm, gik, bik, bd)
    sconv = jnp.transpose(state_conv[layer], (1, 0, 2))
    st_in = state_ssm[layer].reshape(bd, SSD_HEADS * SSD_HEAD_DIM, SSD_STATE)
    y_ssd, conv_s, st_out = _ssd_sample(xbc, z, small, sconv, cw, cbias, dtb, a_pad, dskip_full, gssd, st_in)

    assert n_pages % SAMPLE_SUB_PAGES == 0, "page count must be a multiple of the per-matmul page group"
    idx_pages = math.gcd(IDX_PAGES_PER_STEP, n_pages)
    attn_pages = math.gcd(ATTN_PAGES_PER_STEP, n_pages)
    topk = min(TOPK_MAX, (past + ts) // 4)
    wi3 = small[:, SM_WI:SM_WI + IDX_HEADS].reshape(bd, IDX_HEADS, 1)
    cidx = jnp.transpose(cache_idx_k[layer], (0, 2, 1))
    ck = jnp.transpose(cache_k[layer], (0, 2, 3, 1)).reshape(-1, KV_WIDTH, PAGE_SIZE)
    cv = jnp.transpose(cache_v[layer], (0, 2, 3, 1)).reshape(-1, KV_WIDTH, PAGE_SIZE)
    scores = _idx_scores(page_table, qib.reshape(bd, IDX_HEADS, IDX_DIM), wi3, cidx, idx_pages)
    bias, bias_new = _select_sample(scores.reshape(bd, past), qib, kib, small, topk)
    y_attn = _attn_sample(page_table, qb16.reshape(bd, N_HEADS, HEAD_DIM), bias.reshape(bd, 1, past),
                          kb16.reshape(bd, 1, KV_WIDTH), vb16.reshape(bd, 1, KV_WIDTH),
                          bias_new.reshape(bd, 1, LANES), ck, cv, attn_pages)
    ys = _merge_ffn(xs2, y_ssd, y_attn.reshape(bd, ATTN_WIDTH), ga, gb, wso, wao, wo, gmp, gfp, wgu, wd, gfo, bd)

    y_sample = ys.reshape(bd, ts, D_MODEL)
    k_sample = k.reshape(1, bd, ts, N_KV_HEADS, HEAD_DIM)
    v_sample = v.reshape(1, bd, ts, N_KV_HEADS, HEAD_DIM)
    idx_k_sample = ki.reshape(1, bd, ts, IDX_DIM)
    ssm_sample = st_out.reshape(1, bd, SSD_HEADS, SSD_HEAD_DIM, SSD_STATE)
    conv_sample = jnp.transpose(conv_s, (1, 0, 2))[None]
    return (y_prompt, y_sample, k_prompt, v_prompt, idx_k_prompt, ssm_prompt, conv_prompt,
            k_sample, v_sample, idx_k_sample, ssm_sample, conv_sample)
```

```python
import functools
import math

import numpy as np
import jax
import jax.numpy as jnp
from jax import lax
from jax.experimental import pallas as pl
from jax.experimental.pallas import tpu as pltpu

F32 = jnp.float32
BF16 = jnp.bfloat16
I32 = jnp.int32

D_MODEL = 1024
D_INNER = 2048
SSD_HEAD_DIM = 64
SSD_HEADS = 32
SSD_GROUPS = 4
SSD_STATE = 128
CONV_W = 4
CONV_DIM = D_INNER + 2 * SSD_GROUPS * SSD_STATE
SSD_CHUNK = 128
N_HEADS = 16
N_KV_HEADS = 4
HEAD_DIM = 64
ATTN_WIDTH = N_HEADS * HEAD_DIM
KV_WIDTH = N_KV_HEADS * HEAD_DIM
IDX_HEADS = 8
IDX_DIM = 64
IDX_SCALE = (IDX_HEADS ** -0.5) * (IDX_DIM ** -0.5)
TOPK_MAX = 256
ROPE_THETA = 10000.0
PAGE_SIZE = 128
D_FF = 2816
EPS = 1e-6

LANES = 128
SUBLANES = 8
VMEM_LIMIT = 60 * 1024 * 1024

Z0 = 0
XBC0 = Z0 + D_INNER
Q0 = XBC0 + CONV_DIM
K0 = Q0 + ATTN_WIDTH
V0 = K0 + KV_WIDTH
QI0 = V0 + KV_WIDTH
GA0 = QI0 + IDX_HEADS * IDX_DIM
GB0 = GA0 + D_MODEL
SM0 = GB0 + D_MODEL
N_PROJ = SM0 + LANES
SM_DT = 0
SM_WI = SSD_HEADS
SM_KI = 64

Q_SCALE_LOG2 = (HEAD_DIM ** -0.5) * math.log2(math.e)
NEG_BIG = -1e30
INT_MIN = -(2 ** 31)
F32_MIN_NORMAL_BITS = 0x00800000
TOPK_TRIM_ROUNDS = 2
KEY_BLOCK = 512
PROMPT_Q_BLOCK = LANES
PROMPT_SCORE_BLOCK = 512
PROMPT_KEY_BLOCK = 512
V_ROWS = 80
IDX_PAGES_PER_STEP = 64
ATTN_PAGES_PER_STEP = 32
SAMPLE_SUB_PAGES = 16


def _dot(a, b):
    return jnp.dot(a, b, preferred_element_type=F32)


def _dot_nt(a, b):
    return lax.dot_general(a, b, (((1,), (1,)), ((), ())), preferred_element_type=F32)


def _sigmoid(x):
    return 1.0 / (1.0 + jnp.exp(-x))


def _silu(x):
    return x * _sigmoid(x)


def _softplus(x):
    return jnp.maximum(x, 0.0) + jnp.log1p(jnp.exp(-jnp.abs(x)))


def _rmsnorm(x, g):
    return x * lax.rsqrt(jnp.mean(x * x, axis=-1, keepdims=True) + EPS) * g


def _rope_tile(x, cos, sin_signed, first_half):
    partner = jnp.where(first_half, pltpu.roll(x, LANES - 32, 1), pltpu.roll(x, 32, 1))
    return x * cos + partner * sin_signed


def _rope_wide(x, cos, sin_signed, first_half):
    n = x.shape[1] // LANES
    return jnp.concatenate(
        [_rope_tile(x[:, c * LANES:(c + 1) * LANES], cos, sin_signed, first_half) for c in range(n)], axis=1)


def _positive_zero(x):
    return jnp.where(x == 0.0, 0.0, x)


def _const_spec(shape):
    nd = len(shape)
    return pl.BlockSpec(shape, lambda *_: (0,) * nd, pipeline_mode=pl.Buffered(1))


def _inproj_body(x_ref, cos_ref, sin_ref, g_ref, w_ref, gik_ref, bik_ref,
                 z_ref, xbc_ref, k_ref, v_ref, ga_ref, gb_ref, ki_ref, sm_ref,
                 qb_ref, kb_ref, vb_ref, qib_ref, kib_ref):
    h = _rmsnorm(x_ref[...], g_ref[...]).astype(BF16)
    cos = cos_ref[...]
    sin = sin_ref[...]
    lane = lax.broadcasted_iota(I32, cos.shape, 1)
    first_half = (lane % HEAD_DIM) < (HEAD_DIM // 2)

    def seg(off, n):
        return _dot(h, w_ref[:, off:off + n])

    z_ref[...] = seg(Z0, D_INNER)
    xbc_ref[...] = seg(XBC0, CONV_DIM)
    ga_ref[...] = seg(GA0, D_MODEL)
    gb_ref[...] = seg(GB0, D_MODEL)
    q = _rope_wide(seg(Q0, ATTN_WIDTH), cos, sin, first_half)
    qb_ref[...] = (q * Q_SCALE_LOG2).astype(BF16)
    k = _rope_wide(seg(K0, KV_WIDTH), cos, sin, first_half)
    k_ref[...] = k
    kb_ref[...] = k.astype(BF16)
    v = seg(V0, KV_WIDTH)
    v_ref[...] = v
    vb_ref[...] = v.astype(BF16)
    qib_ref[...] = _rope_wide(seg(QI0, IDX_HEADS * IDX_DIM), cos, sin, first_half).astype(BF16)
    small = seg(SM0, LANES)
    is_ki = lane >= SM_KI
    mu = jnp.sum(jnp.where(is_ki, small, 0.0), axis=-1, keepdims=True) * (1.0 / IDX_DIM)
    dev = jnp.where(is_ki, small - mu, 0.0)
    var = jnp.sum(dev * dev, axis=-1, keepdims=True) * (1.0 / IDX_DIM)
    kin = dev * lax.rsqrt(var + EPS) * gik_ref[...] + bik_ref[...]
    kir = _rope_tile(kin, cos, sin, first_half)
    ki_ref[...] = kir[:, SM_KI:]
    kib_ref[...] = kir[:, SM_KI:].astype(BF16)
    is_wi = (lane >= SM_WI) & (lane < SM_WI + IDX_HEADS)
    sm_ref[...] = jnp.where(is_wi, small * IDX_SCALE, small)


def _in_projection(x2d, cos_tab, sin_tab, tab_index, g, w_perm, gik, bik, tm):
    m = x2d.shape[0]
    row = lambda n: pl.BlockSpec((tm, n), lambda i: (i, 0))
    out_shapes = [
        jax.ShapeDtypeStruct((m, D_INNER), F32),
        jax.ShapeDtypeStruct((m, CONV_DIM), F32),
        jax.ShapeDtypeStruct((m, KV_WIDTH), F32),
        jax.ShapeDtypeStruct((m, KV_WIDTH), F32),
        jax.ShapeDtypeStruct((m, D_MODEL), F32),
        jax.ShapeDtypeStruct((m, D_MODEL), F32),
        jax.ShapeDtypeStruct((m, IDX_DIM), F32),
        jax.ShapeDtypeStruct((m, LANES), F32),
        jax.ShapeDtypeStruct((m, ATTN_WIDTH), BF16),
        jax.ShapeDtypeStruct((m, KV_WIDTH), BF16),
        jax.ShapeDtypeStruct((m, KV_WIDTH), BF16),
        jax.ShapeDtypeStruct((m, IDX_HEADS * IDX_DIM), BF16),
        jax.ShapeDtypeStruct((m, IDX_DIM), BF16),
    ]
    return pl.pallas_call(
        _inproj_body,
        grid=(m // tm,),
        in_specs=[
            row(D_MODEL),
            pl.BlockSpec((tm, LANES), tab_index),
            pl.BlockSpec((tm, LANES), tab_index),
            _const_spec((1, D_MODEL)),
            _const_spec((D_MODEL, N_PROJ)),
            _const_spec((1, LANES)),
            _const_spec((1, LANES)),
        ],
        out_specs=[row(s.shape[1]) for s in out_shapes],
        out_shape=out_shapes,
        compiler_params=pltpu.CompilerParams(dimension_semantics=("arbitrary",), vmem_limit_bytes=VMEM_LIMIT),
        name="in_projection",
    )(x2d, cos_tab, sin_tab, g, w_perm, gik, bik)


def _pair_cols(col_a, col_b, first):
    return jnp.where(first, col_a, col_b)


def _ssd_prompt_body(xbc_ref, z_ref, sm_ref, cw_ref, cbias_ref, dtb_ref, a_ref, dskip_ref, gssd_ref,
                     y_ref, ssm_ref, tail_ref, xp_scr, st_scr, y_scr):
    c = pl.program_id(1)
    q = SSD_CHUNK

    @pl.when(c == 0)
    def _():
        xp_scr[0:SUBLANES, :] = jnp.zeros((SUBLANES, CONV_DIM), F32)
        st_scr[...] = jnp.zeros_like(st_scr)

    xb = xbc_ref[...]
    xp_scr[SUBLANES:SUBLANES + q, :] = xb
    conv = cbias_ref[...]
    for j in range(CONV_W - 1):
        lo = SUBLANES - (CONV_W - 1) + j
        conv = conv + cw_ref[j:j + 1, :] * xp_scr[lo:lo + q, :]
    conv = conv + cw_ref[CONV_W - 1:CONV_W, :] * xb
    xp_scr[SUBLANES - (CONV_W - 1):SUBLANES, :] = xb[q - (CONV_W - 1):q, :]
    tail_ref[0] = xb[q - SUBLANES:q, :]

    act = _silu(conv)
    xs = act[:, :D_INNER]
    gs = SSD_GROUPS * SSD_STATE
    bs = act[:, D_INNER:D_INNER + gs]
    cs = act[:, D_INNER + gs:]

    lane = lax.broadcasted_iota(I32, (q, LANES), 1)
    rowi = lax.broadcasted_iota(I32, (q, LANES), 0)
    causal = rowi >= lane
    first = lane < SSD_HEAD_DIM
    dt = jnp.where(lane < SSD_HEADS, _softplus(sm_ref[...] + dtb_ref[...]), 0.0)
    dta = dt * -jnp.exp(a_ref[...])
    tril = jnp.where(causal, 1.0, 0.0).astype(F32)
    cum = jnp.dot(tril, dta, preferred_element_type=F32, precision=lax.Precision.HIGHEST)
    cum_t = cum.T
    cum_last = cum[q - 1:q, :]
    e_cum = jnp.exp(cum)
    d_last = jnp.exp(cum_last - cum)
    e_last = jnp.exp(cum_last)

    rpg = SSD_HEADS // SSD_GROUPS
    for g in range(SSD_GROUPS):
        cs_g = cs[:, g * SSD_STATE:(g + 1) * SSD_STATE].astype(BF16)
        bs_g = bs[:, g * SSD_STATE:(g + 1) * SSD_STATE]
        cb = _dot_nt(cs_g, bs_g.astype(BF16))
        bs_t = bs_g.T.astype(BF16)
        for pr in range(rpg // 2):
            h0 = g * rpg + 2 * pr
            h1 = h0 + 1
            hp = h0 // 2
            xs_p = xs[:, hp * LANES:(hp + 1) * LANES]
            xdt = xs_p * _pair_cols(dt[:, h0:h0 + 1], dt[:, h1:h1 + 1], first)
            xdt_b = xdt.astype(BF16)
            yd = []
            for h in (h0, h1):
                seg = jnp.where(causal, cum[:, h:h + 1] - cum_t[h:h + 1, :], -jnp.inf)
                m = (cb * jnp.exp(seg)).astype(BF16)
                yd.append(_dot(m, xdt_b))
            st = st_scr[hp]
            y_off = _dot(cs_g, st.astype(BF16)) * _pair_cols(e_cum[:, h0:h0 + 1], e_cum[:, h1:h1 + 1], first)
            y_scr[:, hp * LANES:(hp + 1) * LANES] = jnp.where(first, yd[0], yd[1]) + y_off
            xdl = xdt * _pair_cols(d_last[:, h0:h0 + 1], d_last[:, h1:h1 + 1], first)
            dec = _pair_cols(e_last[:, h0:h0 + 1], e_last[:, h1:h1 + 1], first[0:1, :])
            st_scr[hp] = dec * st + _dot(bs_t, xdl.astype(BF16))

    y = y_scr[...] + dskip_ref[...] * xs
    u = y * _silu(z_ref[...])
    gw = D_INNER // SSD_GROUPS
    for g in range(SSD_GROUPS):
        ug = u[:, g * gw:(g + 1) * gw]
        y_ref[:, g * gw:(g + 1) * gw] = _rmsnorm(ug, gssd_ref[:, g * gw:(g + 1) * gw])

    @pl.when(c == pl.num_programs(1) - 1)
    def _():
        for hp in range(SSD_HEADS // 2):
            st_t = st_scr[hp].T
            ssm_ref[0, 2 * hp] = st_t[:SSD_HEAD_DIM, :]
            ssm_ref[0, 2 * hp + 1] = st_t[SSD_HEAD_DIM:, :]


def _ssd_prompt(xbc, z, small, conv_w, conv_b, dtb_pad, a_pad, dskip_full, g_ssd, bsz, seq):
    nc = seq // SSD_CHUNK
    q = SSD_CHUNK
    tok = lambda n: pl.BlockSpec((q, n), lambda b, c: (b * nc + c, 0))
    return pl.pallas_call(
        _ssd_prompt_body,
        grid=(bsz, nc),
        in_specs=[
            tok(CONV_DIM), tok(D_INNER), tok(LANES),
            _const_spec((CONV_W, CONV_DIM)), _const_spec((1, CONV_DIM)),
            _const_spec((1, LANES)), _const_spec((1, LANES)),
            _const_spec((1, D_INNER)), _const_spec((1, D_INNER)),
        ],
        out_specs=[
            tok(D_INNER),
            pl.BlockSpec((1, SSD_HEADS, SSD_HEAD_DIM, SSD_STATE), lambda b, c: (b, 0, 0, 0)),
            pl.BlockSpec((1, SUBLANES, CONV_DIM), lambda b, c: (b, 0, 0)),
        ],
        out_shape=[
            jax.ShapeDtypeStruct((bsz * seq, D_INNER), F32),
            jax.ShapeDtypeStruct((bsz, SSD_HEADS, SSD_HEAD_DIM, SSD_STATE), F32),
            jax.ShapeDtypeStruct((bsz, SUBLANES, CONV_DIM), F32),
        ],
        scratch_shapes=[
            pltpu.VMEM((SUBLANES + q, CONV_DIM), F32),
            pltpu.VMEM((SSD_HEADS // 2, SSD_STATE, LANES), F32),
            pltpu.VMEM((q, D_INNER), F32),
        ],
        compiler_params=pltpu.CompilerParams(dimension_semantics=("arbitrary", "arbitrary"),
                                             vmem_limit_bytes=VMEM_LIMIT),
        name="ssd_prompt",
    )(xbc, z, small, conv_w, conv_b, dtb_pad, a_pad, dskip_full, g_ssd)


def _float_of_key(t):
    bits = t ^ ((t >> 31) & 0x7FFFFFFF)
    bits = jnp.where((t >= 1) & (t < F32_MIN_NORMAL_BITS), F32_MIN_NORMAL_BITS, bits)
    return pltpu.bitcast(bits, F32)


def _select_topk_lanes(sc_scr, bias_scr, nsb, sbs, k, valid_at, extra=None, extra_bias_ref=None):
    n_total, qb = sc_scr.shape
    acc_rows = 4 * SUBLANES
    k_off = lax.broadcasted_iota(I32, (sbs, qb), 0)
    extra_idx = jnp.full((1, qb), n_total, I32)

    def fold(fn, init):
        return lax.fori_loop(0, nsb, lambda sb, c: fn(pl.multiple_of(sb * sbs, sbs), c), init)

    def count(pred):
        def body(start, acc):
            hit = pred(sc_scr[pl.ds(start, sbs), :], start + k_off).astype(I32)
            return acc + jnp.sum(hit.reshape(sbs // acc_rows, acc_rows, qb), axis=0)
        tot = jnp.sum(fold(body, jnp.zeros((acc_rows, qb), I32)), axis=0, keepdims=True)
        if extra is not None:
            tot = tot + pred(extra, extra_idx).astype(I32)
        return tot

    def smallest(pred):
        def body(start, acc):
            v = sc_scr[pl.ds(start, sbs), :]
            kept = jnp.where(pred(v, start + k_off), v, jnp.inf)
            return jnp.minimum(acc, jnp.min(kept.reshape(sbs // acc_rows, acc_rows, qb), axis=0))
        out = jnp.min(fold(body, jnp.full((acc_rows, qb), jnp.inf, F32)), axis=0, keepdims=True)
        if extra is not None:
            out = jnp.minimum(out, jnp.where(pred(extra, extra_idx), extra, jnp.inf))
        return out

    def write(select):
        def body(start, carry):
            idx = start + k_off
            sel = select(sc_scr[pl.ds(start, sbs), :], idx)
            if valid_at is not None:
                sel = sel & valid_at(idx)
            bias_scr[pl.ds(start, sbs), :] = jnp.where(sel, 0.0, NEG_BIG).astype(F32)
            return carry
        fold(body, 0)
        if extra is not None:
            extra_bias_ref[...] = jnp.where(select(extra, extra_idx), 0.0, NEG_BIG).astype(F32)

    few = count(lambda v, i: v > -jnp.inf) <= k
    def count_ge(t):
        cand = _float_of_key(t)
        return count(lambda v, i: v >= cand)

    zero = jnp.zeros((1, qb), I32)
    t0 = jnp.where(count_ge(zero) >= k, zero, jnp.full((1, qb), INT_MIN, I32))

    def bit_step(j, t):
        cand = t | jnp.left_shift(jnp.int32(1), 30 - j)
        return jnp.where(count_ge(cand) >= k, cand, t)

    thr = jnp.where(few, -jnp.inf, _float_of_key(lax.fori_loop(0, 31, bit_step, t0)))
    cnt = count(lambda v, i: v >= thr)
    over = (cnt > k) & jnp.logical_not(few)
    any_over = jnp.max(over.astype(I32)) > 0

    @pl.when(jnp.logical_not(any_over))
    def _():
        write(lambda v, i: v >= thr)

    @pl.when(any_over)
    def _():
        lo, strict, n_kept = thr, jnp.zeros((1, qb), jnp.bool_), cnt
        kept = lambda lo, strict: (lambda v, i: (v > lo) | ((v == lo) & jnp.logical_not(strict)))
        for _ in range(TOPK_TRIM_ROUNDS):
            m = smallest(kept(lo, strict))
            n_m = count(lambda v, i: v == m)
            drop = over & (n_kept - n_m >= k)
            lo = jnp.where(drop, m, lo)
            strict = strict | drop
            n_kept = jnp.where(drop, n_kept - n_m, n_kept)
        m = smallest(kept(lo, strict))
        tied = over & (n_kept > k)
        need = k - (n_kept - count(lambda v, i: v == m))
        n_bits = n_total.bit_length()

        def idx_step(j, lim):
            cand = lim | jnp.left_shift(jnp.int32(1), n_bits - 1 - j)
            below = count(lambda v, i: (v == m) & (i < cand))
            return jnp.where(below < need, cand, lim)

        last = jnp.where(tied, lax.fori_loop(0, n_bits, idx_step, jnp.zeros((1, qb), I32)), n_total)
        keep = kept(lo, strict)
        write(lambda v, i: keep(v, i) & ((v != m) | (i <= last)))


def _dsa_prompt_body(qi_ref, wit_ref, q_ref, ki_ref, k_ref, vt_ref, o_ref, sc_scr, bias_scr, m_scr, acc_scr,
                     *, topk):
    i = pl.program_id(1)
    qb = PROMPT_Q_BLOCK
    sbs = PROMPT_SCORE_BLOCK
    kbs = PROMPT_KEY_BLOCK
    n_keys = i * qb + qb
    nsb = (n_keys + sbs - 1) // sbs
    nkb = (n_keys + kbs - 1) // kbs
    q_pos = i * qb + lax.broadcasted_iota(I32, (sbs, qb), 1)
    k_off = lax.broadcasted_iota(I32, (sbs, qb), 0)
    qi_stack = jnp.concatenate(
        [qi_ref[:, h * IDX_DIM:(h + 1) * IDX_DIM] for h in range(IDX_HEADS)], axis=0)
    w_rows = wit_ref[...]

    def score_block(sb, carry):
        start = pl.multiple_of(sb * sbs, sbs)
        d = _dot_nt(ki_ref[pl.ds(start, sbs), :], qi_stack)
        acc = jnp.zeros((sbs, qb), F32)
        for h in range(IDX_HEADS):
            acc = acc + jnp.maximum(d[:, h * qb:(h + 1) * qb], 0.0) * w_rows[h:h + 1, :]
        sc = jnp.where(start + k_off <= q_pos, _positive_zero(acc), -jnp.inf)
        sc_scr[pl.ds(start, sbs), :] = sc
        return carry

    lax.fori_loop(0, nsb, score_block, 0)
    _select_topk_lanes(sc_scr, bias_scr, nsb, sbs, topk, lambda idx: idx <= q_pos)

    rep = N_HEADS // N_KV_HEADS
    m_scr[...] = jnp.full(m_scr.shape, NEG_BIG, m_scr.dtype)
    acc_scr[...] = jnp.zeros_like(acc_scr)

    groups = range(N_KV_HEADS)

    def masked_logits(kb):
        start = pl.multiple_of(kb * kbs, kbs)
        bias = jnp.concatenate([bias_scr[pl.ds(start, kbs), :]] * rep, axis=1)
        out = []
        for g in groups:
            qg = jnp.concatenate(
                [q_ref[:, (g * rep + r) * HEAD_DIM:(g * rep + r + 1) * HEAD_DIM] for r in range(rep)], axis=0)
            lg32 = _dot_nt(k_ref[pl.ds(start, kbs), g * HEAD_DIM:(g + 1) * HEAD_DIM], qg) + bias
            out.append(lg32.astype(BF16))
        return out

    def att_block(kb, carry):
        m_old = [m_scr[g] for g in groups]
        acc_old = [acc_scr[g] for g in groups]
        lg = masked_logits(kb)
        m_new = [jnp.maximum(m_old[g], jnp.max(lg[g], axis=0, keepdims=True)) for g in groups]
        p = [jnp.exp2(lg[g] - m_new[g]) for g in groups]
        pv = [_dot(vt_ref[kb, g], p[g]) for g in groups]
        for g in groups:
            alpha = jnp.exp2(m_old[g].astype(F32) - m_new[g].astype(F32))
            acc_scr[g] = alpha * acc_old[g] + pv[g]
            m_scr[g] = m_new[g]
        return carry

    lax.fori_loop(0, nkb, att_block, 0)

    for g in range(N_KV_HEADS):
        acc = acc_scr[g]
        out_t = acc[:HEAD_DIM, :] / acc[HEAD_DIM:HEAD_DIM + 1, :]
        for pr in range(rep // 2):
            pair = jnp.concatenate([out_t[:, (2 * pr) * qb:(2 * pr + 1) * qb],
                                    out_t[:, (2 * pr + 1) * qb:(2 * pr + 2) * qb]], axis=0)
            c0 = (g * rep + 2 * pr) * HEAD_DIM
            o_ref[:, c0:c0 + 2 * HEAD_DIM] = pair.T


def _dsa_prompt(qib, wi_t, qb16, kib, kb16, vt4, bsz, seq):
    topk = min(TOPK_MAX, seq // 4)
    qblk = PROMPT_Q_BLOCK
    kbs = PROMPT_KEY_BLOCK
    nq = seq // qblk
    tok = lambda n: pl.BlockSpec((qblk, n), lambda b, i: (b * nq + i, 0))
    per_seq = lambda n: pl.BlockSpec((seq, n), lambda b, i: (b, 0))
    return pl.pallas_call(
        functools.partial(_dsa_prompt_body, topk=topk),
        grid=(bsz, nq),
        in_specs=[tok(IDX_HEADS * IDX_DIM),
                  pl.BlockSpec((IDX_HEADS, qblk), lambda b, i: (0, b * nq + i)),
                  tok(ATTN_WIDTH),
                  per_seq(IDX_DIM), per_seq(KV_WIDTH),
                  pl.BlockSpec((seq // kbs, N_KV_HEADS, V_ROWS, kbs), lambda b, i: (b, 0, 0, 0))],
        out_specs=tok(ATTN_WIDTH),
        out_shape=jax.ShapeDtypeStruct((bsz * seq, ATTN_WIDTH), F32),
        scratch_shapes=[pltpu.VMEM((seq, qblk), F32), pltpu.VMEM((seq, qblk), F32),
                        pltpu.VMEM((N_KV_HEADS, 1, (N_HEADS // N_KV_HEADS) * qblk), BF16),
                        pltpu.VMEM((N_KV_HEADS, V_ROWS, (N_HEADS // N_KV_HEADS) * qblk), F32)],
        compiler_params=pltpu.CompilerParams(dimension_semantics=("arbitrary", "arbitrary"),
                                             vmem_limit_bytes=VMEM_LIMIT),
        name="dsa_prompt",
    )(qib, wi_t, qb16, kib, kb16, vt4)


def _values_feature_major(vb16, kbs):
    m = vb16.shape[0]
    v3 = vb16.reshape(m, N_KV_HEADS, HEAD_DIM)
    ones = jnp.ones((m, N_KV_HEADS, 1), BF16)
    pad = jnp.zeros((m, N_KV_HEADS, V_ROWS - HEAD_DIM - 1), BF16)
    v_aug = jnp.concatenate([v3, ones, pad], axis=2)
    return jnp.transpose(v_aug.reshape(m // kbs, kbs, N_KV_HEADS, V_ROWS), (0, 2, 3, 1))


def _merge_ffn_body(x_ref, ys_ref, ya_ref, ga_ref, gb_ref, wso_ref, wao_ref, wo_ref, gmp_ref, gfp_ref,
                    wgu_ref, wd_ref, gfo_ref, o_ref):
    mixed = (_sigmoid(ga_ref[...]) * _dot(ys_ref[...].astype(BF16), wso_ref[...])
             + _sigmoid(gb_ref[...]) * _dot(ya_ref[...].astype(BF16), wao_ref[...]))
    x1 = x_ref[...] + _rmsnorm(_dot(mixed.astype(BF16), wo_ref[...]), gmp_ref[...])
    h2 = _rmsnorm(x1, gfp_ref[...]).astype(BF16)
    gate = _dot(h2, wgu_ref[:, :D_FF])
    up = _dot(h2, wgu_ref[:, D_FF:])
    act = (_silu(gate) * up).astype(BF16)
    o_ref[...] = x1 + _rmsnorm(_dot(act, wd_ref[...]), gfo_ref[...])


def _merge_ffn(x2d, y_ssd, y_attn, ga, gb, wso, wao, wo, gmp, gfp, wgu, wd, gfo, tm):
    m = x2d.shape[0]
    row = lambda n: pl.BlockSpec((tm, n), lambda i: (i, 0))
    return pl.pallas_call(
        _merge_ffn_body,
        grid=(m // tm,),
        in_specs=[row(D_MODEL), row(D_INNER), row(ATTN_WIDTH), row(D_MODEL), row(D_MODEL),
                  _const_spec((D_INNER, D_MODEL)), _const_spec((ATTN_WIDTH, D_MODEL)),
                  _const_spec((D_MODEL, D_MODEL)), _const_spec((1, D_MODEL)), _const_spec((1, D_MODEL)),
                  _const_spec((D_MODEL, 2 * D_FF)), _const_spec((D_FF, D_MODEL)), _const_spec((1, D_MODEL))],
        out_specs=row(D_MODEL),
        out_shape=jax.ShapeDtypeStruct((m, D_MODEL), F32),
        compiler_params=pltpu.CompilerParams(dimension_semantics=("arbitrary",), vmem_limit_bytes=VMEM_LIMIT),
        name="merge_ffn",
    )(x2d, y_ssd, y_attn, ga, gb, wso, wao, wo, gmp, gfp, wgu, wd, gfo)


def _ssd_sample_body(xbc_ref, z_ref, sm_ref, sconv_ref, cw_ref, cbias_ref, dtb_ref, a_ref, dskip_ref,
                     gssd_ref, st_ref, y_ref, conv_ref, sto_ref,
                     xs_scr, bs_scr, cs_scr, xt_scr, dtt_scr, dect_scr, yt_scr):
    b = pl.program_id(0)
    nb = pl.num_programs(0)
    hd = SSD_HEAD_DIM
    gs = SSD_GROUPS * SSD_STATE

    @pl.when(b == 0)
    def _():
        xb = xbc_ref[...]
        conv = cbias_ref[...]
        for j in range(CONV_W - 1):
            conv = conv + cw_ref[j:j + 1, :] * sconv_ref[j]
        conv = conv + cw_ref[CONV_W - 1:CONV_W, :] * xb
        for j in range(CONV_W - 2):
            conv_ref[j] = sconv_ref[j + 1]
        conv_ref[CONV_W - 2] = xb
        act = _silu(conv)
        xs = act[:, :D_INNER]
        xs_scr[...] = xs
        bs_scr[...] = act[:, D_INNER:D_INNER + gs]
        cs_scr[...] = act[:, D_INNER + gs:]
        dt = _softplus(sm_ref[...] + dtb_ref[...])
        dec = jnp.exp(dt * -jnp.exp(a_ref[...]))
        dt_full = jnp.concatenate(
            [jnp.broadcast_to(dt[:, h:h + 1], (dt.shape[0], hd)) for h in range(SSD_HEADS)], axis=1)
        dec_full = jnp.concatenate(
            [jnp.broadcast_to(dec[:, h:h + 1], (dt.shape[0], hd)) for h in range(SSD_HEADS)], axis=1)
        xt_scr[...] = xs.T
        dtt_scr[...] = dt_full.T
        dect_scr[...] = dec_full.T
        yt_scr[...] = jnp.zeros_like(yt_scr)

    nbl = xt_scr.shape[1]
    lane_b = lax.broadcasted_iota(I32, (D_INNER, nbl), 1) == b

    def pick(ref):
        return jnp.sum(jnp.where(lane_b, ref[...], 0.0), axis=1, keepdims=True)

    x_col = pick(xt_scr)
    dt_col = pick(dtt_scr)
    dec_col = pick(dect_scr)
    b_row = bs_scr[pl.ds(b, 1), :]
    c_row = cs_scr[pl.ds(b, 1), :]
    rows_pg = (SSD_HEADS // SSD_GROUPS) * hd
    y_cols = []
    for g in range(SSD_GROUPS):
        r0 = g * rows_pg
        hst = st_ref[0, r0:r0 + rows_pg, :]
        bg = b_row[:, g * SSD_STATE:(g + 1) * SSD_STATE]
        cg = c_row[:, g * SSD_STATE:(g + 1) * SSD_STATE]
        hn = dec_col[r0:r0 + rows_pg] * hst + (x_col[r0:r0 + rows_pg] * bg) * dt_col[r0:r0 + rows_pg]
        sto_ref[0, r0:r0 + rows_pg, :] = hn
        y_cols.append(jnp.sum(hn * cg, axis=1, keepdims=True))
    y_col = jnp.concatenate(y_cols, axis=0)
    yt_scr[...] = jnp.where(lane_b, y_col, yt_scr[...])

    @pl.when(b == nb - 1)
    def _():
        y = yt_scr[...].T + dskip_ref[...] * xs_scr[...]
        u = y * _silu(z_ref[...])
        gw = D_INNER // SSD_GROUPS
        for g in range(SSD_GROUPS):
            ug = u[:, g * gw:(g + 1) * gw]
            y_ref[:, g * gw:(g + 1) * gw] = _rmsnorm(ug, gssd_ref[:, g * gw:(g + 1) * gw])


def _ssd_sample(xbc, z, small, sconv, conv_w, conv_b, dtb_pad, a_pad, dskip_full, g_ssd, state):
    nb = xbc.shape[0]
    full = lambda shape: pl.BlockSpec(shape, lambda b: (0,) * len(shape))
    rows = SSD_HEADS * SSD_HEAD_DIM
    st_spec = pl.BlockSpec((1, rows, SSD_STATE), lambda b: (b, 0, 0))
    return pl.pallas_call(
        _ssd_sample_body,
        grid=(nb,),
        in_specs=[full((nb, CONV_DIM)), full((nb, D_INNER)), full((nb, LANES)),
                  full((CONV_W - 1, nb, CONV_DIM)),
                  full((CONV_W, CONV_DIM)), full((1, CONV_DIM)), full((1, LANES)), full((1, LANES)),
                  full((1, D_INNER)), full((1, D_INNER)), st_spec],
        out_specs=[full((nb, D_INNER)), full((CONV_W - 1, nb, CONV_DIM)), st_spec],
        out_shape=[jax.ShapeDtypeStruct((nb, D_INNER), F32),
                   jax.ShapeDtypeStruct((CONV_W - 1, nb, CONV_DIM), F32),
                   jax.ShapeDtypeStruct((nb, rows, SSD_STATE), F32)],
        scratch_shapes=[pltpu.VMEM((nb, D_INNER), F32),
                        pltpu.VMEM((nb, SSD_GROUPS * SSD_STATE), F32),
                        pltpu.VMEM((nb, SSD_GROUPS * SSD_STATE), F32),
                        pltpu.VMEM((D_INNER, nb), F32), pltpu.VMEM((D_INNER, nb), F32),
                        pltpu.VMEM((D_INNER, nb), F32), pltpu.VMEM((D_INNER, nb), F32)],
        compiler_params=pltpu.CompilerParams(dimension_semantics=("arbitrary",), vmem_limit_bytes=VMEM_LIMIT),
        name="ssd_sample",
    )(xbc, z, small, sconv, conv_w, conv_b, dtb_pad, a_pad, dskip_full, g_ssd, state)


class _PagePipeline:
    def __init__(self, pt_ref, pairs, sem, pages, steps_per_seq):
        self.pt_ref, self.pairs, self.sem, self.pages, self.steps_per_seq = pt_ref, pairs, sem, pages, steps_per_seq

    def _copies(self, step, j):
        seq = step // self.steps_per_seq
        chunk = step % self.steps_per_seq
        slot = step % 2
        page = self.pt_ref[seq, chunk * self.pages + j]
        return [pltpu.make_async_copy(src.at[page], buf.at[slot, j], self.sem.at[slot]) for src, buf in self.pairs]

    def start(self, step):
        def body(j, carry):
            for cp in self._copies(step, j):
                cp.start()
            return carry
        lax.fori_loop(0, self.pages, body, 0)

    def wait(self, step):
        slot = step % 2
        for src, buf in self.pairs:
            pltpu.make_async_copy(src.at[pl.ds(0, self.pages)], buf.at[slot], self.sem.at[slot]).wait()

    def advance(self, step, n_steps):
        @pl.when(step == 0)
        def _():
            self.start(step)

        @pl.when(step + 1 < n_steps)
        def _():
            self.start(step + 1)

        self.wait(step)


def _idx_scores_body(pt_ref, qi_ref, wi_ref, cache_hbm, o_ref, buf, sem, *, pages):
    step = pl.program_id(0) * pl.num_programs(1) + pl.program_id(1)
    n_steps = pl.num_programs(0) * pl.num_programs(1)
    _PagePipeline(pt_ref, [(cache_hbm, buf)], sem, pages, pl.num_programs(1)).advance(step, n_steps)
    slot = step % 2
    qi = qi_ref[0]
    wi = wi_ref[0]
    sub = SAMPLE_SUB_PAGES
    for s in range(pages // sub):
        keys_t = jnp.concatenate([buf[slot, s * sub + j] for j in range(sub)], axis=1).astype(BF16)
        d = _dot(qi, keys_t)
        o_ref[0, :, s * sub * PAGE_SIZE:(s + 1) * sub * PAGE_SIZE] = jnp.sum(
            jnp.maximum(d, 0.0) * wi, axis=0, keepdims=True)


def _idx_scores(page_table, qi3, wi3, cache_idx, pages):
    nb, n_pages = page_table.shape
    nch = n_pages // pages
    span = pages * PAGE_SIZE
    grid_spec = pltpu.PrefetchScalarGridSpec(
        num_scalar_prefetch=1,
        grid=(nb, nch),
        in_specs=[pl.BlockSpec((1, IDX_HEADS, IDX_DIM), lambda b, c, pt: (b, 0, 0)),
                  pl.BlockSpec((1, IDX_HEADS, 1), lambda b, c, pt: (b, 0, 0)),
                  pl.BlockSpec(memory_space=pl.ANY)],
        out_specs=pl.BlockSpec((1, 1, span), lambda b, c, pt: (b, 0, c)),
        scratch_shapes=[pltpu.VMEM((2, pages, IDX_DIM, PAGE_SIZE), F32), pltpu.SemaphoreType.DMA((2,))],
    )
    return pl.pallas_call(
        functools.partial(_idx_scores_body, pages=pages),
        grid_spec=grid_spec,
        out_shape=jax.ShapeDtypeStruct((nb, 1, n_pages * PAGE_SIZE), F32),
        compiler_params=pltpu.CompilerParams(dimension_semantics=("arbitrary", "arbitrary"),
                                             vmem_limit_bytes=VMEM_LIMIT),
        name="idx_scores_sample",
    )(page_table, qi3, wi3, cache_idx)


def _select_sample_body(sct_ref, qi_ref, ki_ref, sm_ref, bias_ref, biasn_ref, sc_scr, *, topk):
    length, rows = sct_ref.shape
    sbs = KEY_BLOCK
    for sb in range(length // sbs):
        sc_scr[sb * sbs:(sb + 1) * sbs, :] = _positive_zero(sct_ref[sb * sbs:(sb + 1) * sbs, :])
    ki = ki_ref[...].astype(F32)
    sm = sm_ref[...]
    sc_new = jnp.zeros((rows, 1), F32)
    for h in range(IDX_HEADS):
        d = jnp.sum(qi_ref[:, h * IDX_DIM:(h + 1) * IDX_DIM].astype(F32) * ki, axis=1, keepdims=True)
        sc_new = sc_new + jnp.maximum(d, 0.0) * sm[:, SM_WI + h:SM_WI + h + 1]
    new_row = jnp.broadcast_to(_positive_zero(sc_new), (rows, rows)).T[0:1, :]
    _select_topk_lanes(sc_scr, bias_ref, length // sbs, sbs, topk, None, extra=new_row, extra_bias_ref=biasn_ref)


def _select_sample(scores_t, qib, kib, small, topk):
    length, nb = scores_t.shape
    return pl.pallas_call(
        functools.partial(_select_sample_body, topk=topk),
        out_shape=[jax.ShapeDtypeStruct((length, nb), F32), jax.ShapeDtypeStruct((1, nb), F32)],
        scratch_shapes=[pltpu.VMEM((length, nb), F32)],
        compiler_params=pltpu.CompilerParams(vmem_limit_bytes=VMEM_LIMIT),
        name="select_sample",
    )(scores_t, qib, kib, small)


def _attn_sample_body(pt_ref, q_ref, bias_ref, kn_ref, vn_ref, biasn_ref, ck_hbm, cv_hbm, o_ref,
                      kbuf, vbuf, sem, m_scr, l_scr, acc_scr, *, pages):
    c = pl.program_id(1)
    step = pl.program_id(0) * pl.num_programs(1) + c
    n_steps = pl.num_programs(0) * pl.num_programs(1)
    _PagePipeline(pt_ref, [(ck_hbm, kbuf), (cv_hbm, vbuf)], sem, pages, pl.num_programs(1)).advance(step, n_steps)
    slot = step % 2
    rep = N_HEADS // N_KV_HEADS

    @pl.when(c == 0)
    def _():
        m_scr[...] = jnp.full(m_scr.shape, NEG_BIG, F32)
        l_scr[...] = jnp.zeros_like(l_scr)
        acc_scr[...] = jnp.zeros_like(acc_scr)

    q = q_ref[0]
    head = lax.broadcasted_iota(I32, (N_HEADS, KV_WIDTH), 0)
    lane = lax.broadcasted_iota(I32, (N_HEADS, KV_WIDTH), 1)
    own = (lane // HEAD_DIM) == (head // rep)
    q_bd = jnp.where(own, jnp.concatenate([q.astype(F32)] * N_KV_HEADS, axis=1), 0.0).astype(BF16)
    sub = SAMPLE_SUB_PAGES
    for s in range(pages // sub):
        kk_t = jnp.concatenate([kbuf[slot, s * sub + j] for j in range(sub)], axis=1).astype(BF16)
        vv_t = jnp.concatenate([vbuf[slot, s * sub + j] for j in range(sub)], axis=1).astype(BF16)
        lg = _dot(q_bd, kk_t) + bias_ref[0, :, s * sub * PAGE_SIZE:(s + 1) * sub * PAGE_SIZE]
        m_old = m_scr[...]
        m_new = jnp.maximum(m_old, jnp.max(lg, axis=1, keepdims=True))
        alpha = jnp.exp2(m_old - m_new)
        p = jnp.exp2(lg - m_new)
        l_scr[...] = alpha * l_scr[...] + jnp.sum(p, axis=1, keepdims=True)
        acc_scr[...] = alpha * acc_scr[...] + _dot_nt(p.astype(BF16), vv_t)
        m_scr[...] = m_new

    @pl.when(c == pl.num_programs(1) - 1)
    def _():
        kn = kn_ref[0].astype(F32)
        lg_n = jnp.sum(q_bd.astype(F32) * kn, axis=1, keepdims=True) + biasn_ref[0][:, 0:1]
        m_o = m_scr[...]
        m_n = jnp.maximum(m_o, lg_n)
        al = jnp.exp2(m_o - m_n)
        pn = jnp.exp2(lg_n - m_n)
        l_fin = al * l_scr[...] + pn
        acc = al * acc_scr[...] + pn.astype(BF16).astype(F32) * vn_ref[0].astype(F32)
        res = jnp.where(own, acc / l_fin, 0.0)
        out = res[:, 0:HEAD_DIM]
        for g in range(1, N_KV_HEADS):
            out = out + res[:, g * HEAD_DIM:(g + 1) * HEAD_DIM]
        o_ref[0] = out


def _attn_sample(page_table, q3, bias3, kn3, vn3, biasn3, cache_k, cache_v, pages):
    nb, n_pages = page_table.shape
    nch = n_pages // pages
    span = pages * PAGE_SIZE
    per_b = lambda s: pl.BlockSpec((1,) + s, lambda b, c, pt: (b, 0, 0))
    grid_spec = pltpu.PrefetchScalarGridSpec(
        num_scalar_prefetch=1,
        grid=(nb, nch),
        in_specs=[per_b((N_HEADS, HEAD_DIM)),
                  pl.BlockSpec((1, 1, span), lambda b, c, pt: (b, 0, c)),
                  per_b((1, KV_WIDTH)), per_b((1, KV_WIDTH)), per_b((1, LANES)),
                  pl.BlockSpec(memory_space=pl.ANY), pl.BlockSpec(memory_space=pl.ANY)],
        out_specs=per_b((N_HEADS, HEAD_DIM)),
        scratch_shapes=[pltpu.VMEM((2, pages, KV_WIDTH, PAGE_SIZE), F32),
                        pltpu.VMEM((2, pages, KV_WIDTH, PAGE_SIZE), F32),
                        pltpu.SemaphoreType.DMA((2,)),
                        pltpu.VMEM((N_HEADS, 1), F32), pltpu.VMEM((N_HEADS, 1), F32),
                        pltpu.VMEM((N_HEADS, KV_WIDTH), F32)],
    )
    return pl.pallas_call(
        functools.partial(_attn_sample_body, pages=pages),
        grid_spec=grid_spec,
        out_shape=jax.ShapeDtypeStruct((nb, N_HEADS, HEAD_DIM), F32),
        compiler_params=pltpu.CompilerParams(dimension_semantics=("arbitrary", "arbitrary"),
                                             vmem_limit_bytes=VMEM_LIMIT),
        name="attn_sample",
    )(page_table, q3, bias3, kn3, vn3, biasn3, cache_k, cache_v)


def _rope_tables(pos):
    half = HEAD_DIM // 2
    inv = ROPE_THETA ** (-(jnp.arange(half, dtype=F32) * 2.0) / HEAD_DIM)
    ang = pos.astype(F32)[:, None] * inv[None, :]
    cos = jnp.cos(ang)
    sin = jnp.sin(ang)
    cos_t = jnp.concatenate([cos, cos, cos, cos], axis=1)
    sin_t = jnp.concatenate([-sin, sin, -sin, sin], axis=1)
    return cos_t, sin_t


def _permute_w_in(w):
    sizes = (D_INNER, CONV_DIM, SSD_HEADS, ATTN_WIDTH, KV_WIDTH, KV_WIDTH,
             IDX_HEADS * IDX_DIM, IDX_DIM, IDX_HEADS, D_MODEL, D_MODEL)
    cuts = np.concatenate([[0], np.cumsum(sizes)])
    z, xbc, dt, q, k, v, qi, ki, wi, ga, gb = [w[:, int(cuts[j]):int(cuts[j + 1])] for j in range(len(sizes))]
    pad = jnp.zeros((w.shape[0], SM_KI - SM_WI - IDX_HEADS), w.dtype)
    return jnp.concatenate([z, xbc, q, k, v, qi, ga, gb, dt, wi, pad, ki], axis=1).astype(BF16)


def _pad_lanes(v, offset=0):
    out = jnp.zeros((1, LANES), F32)
    return out.at[0, offset:offset + v.shape[0]].set(v.astype(F32))


def kernel(x_prompt, x_sample, cache_k, cache_v, cache_idx_k, state_ssm, state_conv, page_table, g_mix_pre, w_in,
           g_idx_k, b_idx_k, conv_w, conv_b, dt_bias, a_log, d_skip, g_ssd, w_ssd_out, w_attn_out, w_o, g_mix_post,
           g_ffn_pre, w_gate_up, w_down, g_ffn_post):
    bp, sp, _ = x_prompt.shape
    bd, ts, _ = x_sample.shape
    assert ts == 1 and w_in.shape[0] == 1, "one decode token per sample sequence, depth 1"
    n_pages = page_table.shape[1]
    past = n_pages * PAGE_SIZE
    layer = 0

    w_perm = _permute_w_in(w_in[layer])
    g_pre = g_mix_pre[layer][None, :]
    gik = _pad_lanes(g_idx_k[layer], SM_KI)
    bik = _pad_lanes(b_idx_k[layer], SM_KI)
    dtb = _pad_lanes(dt_bias[layer])
    a_pad = _pad_lanes(a_log[layer])
    dskip_full = jnp.repeat(d_skip[layer].astype(F32), SSD_HEAD_DIM)[None, :]
    gssd = g_ssd[layer][None, :]
    cw = conv_w[layer]
    cbias = conv_b[layer][None, :]
    wso = w_ssd_out[layer].astype(BF16)
    wao = w_attn_out[layer].astype(BF16)
    wo = w_o[layer].astype(BF16)
    wgu = w_gate_up[layer].astype(BF16)
    wd = w_down[layer].astype(BF16)
    gmp = g_mix_post[layer][None, :]
    gfp = g_ffn_pre[layer][None, :]
    gfo = g_ffn_post[layer][None, :]

    tm = 256
    xp2 = x_prompt.reshape(bp * sp, D_MODEL)
    cos_p, sin_p = _rope_tables(jnp.arange(sp))
    tiles_per_seq = sp // tm
    (z, xbc, k, v, ga, gb, ki, small, qb16, kb16, vb16, qib, kib) = _in_projection(
        xp2, cos_p, sin_p, lambda i: (i % tiles_per_seq, 0), g_pre, w_perm, gik, bik, tm)
    y_ssd, ssm_p, tail = _ssd_prompt(xbc, z, small, cw, cbias, dtb, a_pad, dskip_full, gssd, bp, sp)
    wi_t = small[:, SM_WI:SM_WI + IDX_HEADS].T
    y_attn = _dsa_prompt(qib, wi_t, qb16, kib, kb16, _values_feature_major(vb16, PROMPT_KEY_BLOCK), bp, sp)
    yp = _merge_ffn(xp2, y_ssd, y_attn, ga, gb, wso, wao, wo, gmp, gfp, wgu, wd, gfo, tm)

    y_prompt = yp.reshape(bp, sp, D_MODEL)
    k_prompt = k.reshape(1, bp, sp, N_KV_HEADS, HEAD_DIM)
    v_prompt = v.reshape(1, bp, sp, N_KV_HEADS, HEAD_DIM)
    idx_k_prompt = ki.reshape(1, bp, sp, IDX_DIM)
    ssm_prompt = ssm_p[None]
    conv_prompt = tail[None, :, SUBLANES - (CONV_W - 1):, :]

    xs2 = x_sample.reshape(bd, D_MODEL)
    cos_s, sin_s = _rope_tables(jnp.full((bd,), past, jnp.int32))
    (z, xbc, k, v, ga, gb, ki, small, qb16, kb16, vb16, qib, kib) = _in_projection(
        xs2, cos_s, sin_s, lambda i: (i, 0), g_pre, w_perm, gik, bik, bd)
    sconv = jnp.transpose(state_conv[layer], (1, 0, 2))
    st_in = state_ssm[layer].reshape(bd, SSD_HEADS * SSD_HEAD_DIM, SSD_STATE)
    y_ssd, conv_s, st_out = _ssd_sample(xbc, z, small, sconv, cw, cbias, dtb, a_pad, dskip_full, gssd, st_in)

    assert n_pages % SAMPLE_SUB_PAGES == 0, "page count must be a multiple of the per-matmul page group"
    idx_pages = math.gcd(IDX_PAGES_PER_STEP, n_pages)
    attn_pages = math.gcd(ATTN_PAGES_PER_STEP, n_pages)
    topk = min(TOPK_MAX, (past + ts) // 4)
    wi3 = small[:, SM_WI:SM_WI + IDX_HEADS].reshape(bd, IDX_HEADS, 1)
    cidx = jnp.transpose(cache_idx_k[layer], (0, 2, 1))
    ck = jnp.transpose(cache_k[layer], (0, 2, 3, 1)).reshape(-1, KV_WIDTH, PAGE_SIZE)
    cv = jnp.transpose(cache_v[layer], (0, 2, 3, 1)).reshape(-1, KV_WIDTH, PAGE_SIZE)
    scores = _idx_scores(page_table, qib.reshape(bd, IDX_HEADS, IDX_DIM), wi3, cidx, idx_pages)
    bias_t, bias_new = _select_sample(scores.reshape(bd, past).T, qib, kib, small, topk)
    bias_new3 = jnp.broadcast_to(bias_new.reshape(bd, 1, 1), (bd, 1, LANES))
    y_attn = _attn_sample(page_table, qb16.reshape(bd, N_HEADS, HEAD_DIM), bias_t.T.reshape(bd, 1, past),
                          kb16.reshape(bd, 1, KV_WIDTH), vb16.reshape(bd, 1, KV_WIDTH),
                          bias_new3, ck, cv, attn_pages)
    ys = _merge_ffn(xs2, y_ssd, y_attn.reshape(bd, ATTN_WIDTH), ga, gb, wso, wao, wo, gmp, gfp, wgu, wd, gfo, bd)

    y_sample = ys.reshape(bd, ts, D_MODEL)
    k_sample = k.reshape(1, bd, ts, N_KV_HEADS, HEAD_DIM)
    v_sample = v.reshape(1, bd, ts, N_KV_HEADS, HEAD_DIM)
    idx_k_sample = ki.reshape(1, bd, ts, IDX_DIM)
    ssm_sample = st_out.reshape(1, bd, SSD_HEADS, SSD_HEAD_DIM, SSD_STATE)
    conv_sample = jnp.transpose(conv_s, (1, 0, 2))[None]
    return (y_prompt, y_sample, k_prompt, v_prompt, idx_k_prompt, ssm_prompt, conv_prompt,
            k_sample, v_sample, idx_k_sample, ssm_sample, conv_sample)
```

```python
import functools
import math

import numpy as np
import jax
import jax.numpy as jnp
from jax import lax
from jax.experimental import pallas as pl
from jax.experimental.pallas import tpu as pltpu

F32 = jnp.float32
BF16 = jnp.bfloat16
I32 = jnp.int32

D_MODEL = 1024
D_INNER = 2048
SSD_HEAD_DIM = 64
SSD_HEADS = 32
SSD_GROUPS = 4
SSD_STATE = 128
CONV_W = 4
CONV_DIM = D_INNER + 2 * SSD_GROUPS * SSD_STATE
SSD_CHUNK = 128
N_HEADS = 16
N_KV_HEADS = 4
HEAD_DIM = 64
ATTN_WIDTH = N_HEADS * HEAD_DIM
KV_WIDTH = N_KV_HEADS * HEAD_DIM
IDX_HEADS = 8
IDX_DIM = 64
IDX_SCALE = (IDX_HEADS ** -0.5) * (IDX_DIM ** -0.5)
TOPK_MAX = 256
ROPE_THETA = 10000.0
PAGE_SIZE = 128
D_FF = 2816
EPS = 1e-6

LANES = 128
SUBLANES = 8
VMEM_LIMIT = 60 * 1024 * 1024

Z0 = 0
XBC0 = Z0 + D_INNER
Q0 = XBC0 + CONV_DIM
K0 = Q0 + ATTN_WIDTH
V0 = K0 + KV_WIDTH
QI0 = V0 + KV_WIDTH
GA0 = QI0 + IDX_HEADS * IDX_DIM
GB0 = GA0 + D_MODEL
SM0 = GB0 + D_MODEL
N_PROJ = SM0 + LANES
SM_DT = 0
SM_WI = SSD_HEADS
SM_KI = 64

Q_SCALE_LOG2 = (HEAD_DIM ** -0.5) * math.log2(math.e)
NEG_BIG = -1e30
INT_MIN = -(2 ** 31)
F32_MIN_NORMAL_BITS = 0x00800000
TOPK_TRIM_ROUNDS = 2
KEY_BLOCK = 512
PROMPT_Q_BLOCK = 2 * LANES
PROMPT_SCORE_BLOCK = 512
PROMPT_KEY_BLOCK = 512
V_ROWS = 80
IDX_PAGES_PER_STEP = 64
ATTN_PAGES_PER_STEP = 32
SAMPLE_SUB_PAGES = 16


def _dot(a, b):
    return jnp.dot(a, b, preferred_element_type=F32)


def _dot_nt(a, b):
    return lax.dot_general(a, b, (((1,), (1,)), ((), ())), preferred_element_type=F32)


def _sigmoid(x):
    return 1.0 / (1.0 + jnp.exp(-x))


def _silu(x):
    return x * _sigmoid(x)


def _softplus(x):
    return jnp.maximum(x, 0.0) + jnp.log1p(jnp.exp(-jnp.abs(x)))


def _rmsnorm(x, g):
    return x * lax.rsqrt(jnp.mean(x * x, axis=-1, keepdims=True) + EPS) * g


def _rope_tile(x, cos, sin_signed, first_half):
    partner = jnp.where(first_half, pltpu.roll(x, LANES - 32, 1), pltpu.roll(x, 32, 1))
    return x * cos + partner * sin_signed


def _rope_wide(x, cos, sin_signed, first_half):
    n = x.shape[1] // LANES
    return jnp.concatenate(
        [_rope_tile(x[:, c * LANES:(c + 1) * LANES], cos, sin_signed, first_half) for c in range(n)], axis=1)


def _positive_zero(x):
    return jnp.where(x == 0.0, 0.0, x)


def _const_spec(shape):
    nd = len(shape)
    return pl.BlockSpec(shape, lambda *_: (0,) * nd, pipeline_mode=pl.Buffered(1))


def _inproj_body(x_ref, cos_ref, sin_ref, g_ref, w_ref, gik_ref, bik_ref,
                 z_ref, xbc_ref, k_ref, v_ref, ga_ref, gb_ref, ki_ref, sm_ref,
                 qb_ref, kb_ref, vb_ref, qib_ref, kib_ref):
    h = _rmsnorm(x_ref[...], g_ref[...]).astype(BF16)
    cos = cos_ref[...]
    sin = sin_ref[...]
    lane = lax.broadcasted_iota(I32, cos.shape, 1)
    first_half = (lane % HEAD_DIM) < (HEAD_DIM // 2)

    def seg(off, n):
        return _dot(h, w_ref[:, off:off + n])

    z_ref[...] = seg(Z0, D_INNER)
    xbc_ref[...] = seg(XBC0, CONV_DIM)
    ga_ref[...] = seg(GA0, D_MODEL)
    gb_ref[...] = seg(GB0, D_MODEL)
    q = _rope_wide(seg(Q0, ATTN_WIDTH), cos, sin, first_half)
    qb_ref[...] = (q * Q_SCALE_LOG2).astype(BF16)
    k = _rope_wide(seg(K0, KV_WIDTH), cos, sin, first_half)
    k_ref[...] = k
    kb_ref[...] = k.astype(BF16)
    v = seg(V0, KV_WIDTH)
    v_ref[...] = v
    vb_ref[...] = v.astype(BF16)
    qib_ref[...] = _rope_wide(seg(QI0, IDX_HEADS * IDX_DIM), cos, sin, first_half).astype(BF16)
    small = seg(SM0, LANES)
    is_ki = lane >= SM_KI
    mu = jnp.sum(jnp.where(is_ki, small, 0.0), axis=-1, keepdims=True) * (1.0 / IDX_DIM)
    dev = jnp.where(is_ki, small - mu, 0.0)
    var = jnp.sum(dev * dev, axis=-1, keepdims=True) * (1.0 / IDX_DIM)
    kin = dev * lax.rsqrt(var + EPS) * gik_ref[...] + bik_ref[...]
    kir = _rope_tile(kin, cos, sin, first_half)
    ki_ref[...] = kir[:, SM_KI:]
    kib_ref[...] = kir[:, SM_KI:].astype(BF16)
    is_wi = (lane >= SM_WI) & (lane < SM_WI + IDX_HEADS)
    sm_ref[...] = jnp.where(is_wi, small * IDX_SCALE, small)


def _in_projection(x2d, cos_tab, sin_tab, tab_index, g, w_perm, gik, bik, tm):
    m = x2d.shape[0]
    row = lambda n: pl.BlockSpec((tm, n), lambda i: (i, 0))
    out_shapes = [
        jax.ShapeDtypeStruct((m, D_INNER), F32),
        jax.ShapeDtypeStruct((m, CONV_DIM), F32),
        jax.ShapeDtypeStruct((m, KV_WIDTH), F32),
        jax.ShapeDtypeStruct((m, KV_WIDTH), F32),
        jax.ShapeDtypeStruct((m, D_MODEL), F32),
        jax.ShapeDtypeStruct((m, D_MODEL), F32),
        jax.ShapeDtypeStruct((m, IDX_DIM), F32),
        jax.ShapeDtypeStruct((m, LANES), F32),
        jax.ShapeDtypeStruct((m, ATTN_WIDTH), BF16),
        jax.ShapeDtypeStruct((m, KV_WIDTH), BF16),
        jax.ShapeDtypeStruct((m, KV_WIDTH), BF16),
        jax.ShapeDtypeStruct((m, IDX_HEADS * IDX_DIM), BF16),
        jax.ShapeDtypeStruct((m, IDX_DIM), BF16),
    ]
    return pl.pallas_call(
        _inproj_body,
        grid=(m // tm,),
        in_specs=[
            row(D_MODEL),
            pl.BlockSpec((tm, LANES), tab_index),
            pl.BlockSpec((tm, LANES), tab_index),
            _const_spec((1, D_MODEL)),
            _const_spec((D_MODEL, N_PROJ)),
            _const_spec((1, LANES)),
            _const_spec((1, LANES)),
        ],
        out_specs=[row(s.shape[1]) for s in out_shapes],
        out_shape=out_shapes,
        compiler_params=pltpu.CompilerParams(dimension_semantics=("arbitrary",), vmem_limit_bytes=VMEM_LIMIT),
        name="in_projection",
    )(x2d, cos_tab, sin_tab, g, w_perm, gik, bik)


def _pair_cols(col_a, col_b, first):
    return jnp.where(first, col_a, col_b)


def _ssd_prompt_body(xbc_ref, z_ref, sm_ref, cw_ref, cbias_ref, dtb_ref, a_ref, dskip_ref, gssd_ref,
                     y_ref, ssm_ref, tail_ref, xp_scr, st_scr, y_scr):
    c = pl.program_id(1)
    q = SSD_CHUNK

    @pl.when(c == 0)
    def _():
        xp_scr[0:SUBLANES, :] = jnp.zeros((SUBLANES, CONV_DIM), F32)
        st_scr[...] = jnp.zeros_like(st_scr)

    xb = xbc_ref[...]
    xp_scr[SUBLANES:SUBLANES + q, :] = xb
    conv = cbias_ref[...]
    for j in range(CONV_W - 1):
        lo = SUBLANES - (CONV_W - 1) + j
        conv = conv + cw_ref[j:j + 1, :] * xp_scr[lo:lo + q, :]
    conv = conv + cw_ref[CONV_W - 1:CONV_W, :] * xb
    xp_scr[SUBLANES - (CONV_W - 1):SUBLANES, :] = xb[q - (CONV_W - 1):q, :]
    tail_ref[0] = xb[q - SUBLANES:q, :]

    act = _silu(conv)
    xs = act[:, :D_INNER]
    gs = SSD_GROUPS * SSD_STATE
    bs = act[:, D_INNER:D_INNER + gs]
    cs = act[:, D_INNER + gs:]

    lane = lax.broadcasted_iota(I32, (q, LANES), 1)
    rowi = lax.broadcasted_iota(I32, (q, LANES), 0)
    causal = rowi >= lane
    first = lane < SSD_HEAD_DIM
    dt = jnp.where(lane < SSD_HEADS, _softplus(sm_ref[...] + dtb_ref[...]), 0.0)
    dta = dt * -jnp.exp(a_ref[...])
    tril = jnp.where(causal, 1.0, 0.0).astype(F32)
    cum = jnp.dot(tril, dta, preferred_element_type=F32, precision=lax.Precision.HIGHEST)
    cum_t = cum.T
    cum_last = cum[q - 1:q, :]
    e_cum = jnp.exp(cum)
    d_last = jnp.exp(cum_last - cum)
    e_last = jnp.exp(cum_last)

    rpg = SSD_HEADS // SSD_GROUPS
    for g in range(SSD_GROUPS):
        cs_g = cs[:, g * SSD_STATE:(g + 1) * SSD_STATE].astype(BF16)
        bs_g = bs[:, g * SSD_STATE:(g + 1) * SSD_STATE]
        cb = _dot_nt(cs_g, bs_g.astype(BF16))
        bs_t = bs_g.T.astype(BF16)
        for pr in range(rpg // 2):
            h0 = g * rpg + 2 * pr
            h1 = h0 + 1
            hp = h0 // 2
            xs_p = xs[:, hp * LANES:(hp + 1) * LANES]
            xdt = xs_p * _pair_cols(dt[:, h0:h0 + 1], dt[:, h1:h1 + 1], first)
            xdt_b = xdt.astype(BF16)
            yd = []
            for h in (h0, h1):
                seg = jnp.where(causal, cum[:, h:h + 1] - cum_t[h:h + 1, :], -jnp.inf)
                m = (cb * jnp.exp(seg)).astype(BF16)
                yd.append(_dot(m, xdt_b))
            st = st_scr[hp]
            y_off = _dot(cs_g, st.astype(BF16)) * _pair_cols(e_cum[:, h0:h0 + 1], e_cum[:, h1:h1 + 1], first)
            y_scr[:, hp * LANES:(hp + 1) * LANES] = jnp.where(first, yd[0], yd[1]) + y_off
            xdl = xdt * _pair_cols(d_last[:, h0:h0 + 1], d_last[:, h1:h1 + 1], first)
            dec = _pair_cols(e_last[:, h0:h0 + 1], e_last[:, h1:h1 + 1], first[0:1, :])
            st_scr[hp] = dec * st + _dot(bs_t, xdl.astype(BF16))

    y = y_scr[...] + dskip_ref[...] * xs
    u = y * _silu(z_ref[...])
    gw = D_INNER // SSD_GROUPS
    for g in range(SSD_GROUPS):
        ug = u[:, g * gw:(g + 1) * gw]
        y_ref[:, g * gw:(g + 1) * gw] = _rmsnorm(ug, gssd_ref[:, g * gw:(g + 1) * gw])

    @pl.when(c == pl.num_programs(1) - 1)
    def _():
        for hp in range(SSD_HEADS // 2):
            st_t = st_scr[hp].T
            ssm_ref[0, 2 * hp] = st_t[:SSD_HEAD_DIM, :]
            ssm_ref[0, 2 * hp + 1] = st_t[SSD_HEAD_DIM:, :]


def _ssd_prompt(xbc, z, small, conv_w, conv_b, dtb_pad, a_pad, dskip_full, g_ssd, bsz, seq):
    nc = seq // SSD_CHUNK
    q = SSD_CHUNK
    tok = lambda n: pl.BlockSpec((q, n), lambda b, c: (b * nc + c, 0))
    return pl.pallas_call(
        _ssd_prompt_body,
        grid=(bsz, nc),
        in_specs=[
            tok(CONV_DIM), tok(D_INNER), tok(LANES),
            _const_spec((CONV_W, CONV_DIM)), _const_spec((1, CONV_DIM)),
            _const_spec((1, LANES)), _const_spec((1, LANES)),
            _const_spec((1, D_INNER)), _const_spec((1, D_INNER)),
        ],
        out_specs=[
            tok(D_INNER),
            pl.BlockSpec((1, SSD_HEADS, SSD_HEAD_DIM, SSD_STATE), lambda b, c: (b, 0, 0, 0)),
            pl.BlockSpec((1, SUBLANES, CONV_DIM), lambda b, c: (b, 0, 0)),
        ],
        out_shape=[
            jax.ShapeDtypeStruct((bsz * seq, D_INNER), F32),
            jax.ShapeDtypeStruct((bsz, SSD_HEADS, SSD_HEAD_DIM, SSD_STATE), F32),
            jax.ShapeDtypeStruct((bsz, SUBLANES, CONV_DIM), F32),
        ],
        scratch_shapes=[
            pltpu.VMEM((SUBLANES + q, CONV_DIM), F32),
            pltpu.VMEM((SSD_HEADS // 2, SSD_STATE, LANES), F32),
            pltpu.VMEM((q, D_INNER), F32),
        ],
        compiler_params=pltpu.CompilerParams(dimension_semantics=("arbitrary", "arbitrary"),
                                             vmem_limit_bytes=VMEM_LIMIT),
        name="ssd_prompt",
    )(xbc, z, small, conv_w, conv_b, dtb_pad, a_pad, dskip_full, g_ssd)


def _float_of_key(t):
    bits = t ^ ((t >> 31) & 0x7FFFFFFF)
    bits = jnp.where((t >= 1) & (t < F32_MIN_NORMAL_BITS), F32_MIN_NORMAL_BITS, bits)
    return pltpu.bitcast(bits, F32)


def _select_topk_lanes(sc_scr, bias_scr, nsb, sbs, k, valid_at, extra=None, extra_bias_ref=None):
    n_total, qb = sc_scr.shape
    acc_rows = 4 * SUBLANES
    k_off = lax.broadcasted_iota(I32, (sbs, qb), 0)
    extra_idx = jnp.full((1, qb), n_total, I32)

    def fold(fn, init):
        return lax.fori_loop(0, nsb, lambda sb, c: fn(pl.multiple_of(sb * sbs, sbs), c), init)

    def count(pred):
        def body(start, acc):
            hit = pred(sc_scr[pl.ds(start, sbs), :], start + k_off).astype(I32)
            return acc + jnp.sum(hit.reshape(sbs // acc_rows, acc_rows, qb), axis=0)
        tot = jnp.sum(fold(body, jnp.zeros((acc_rows, qb), I32)), axis=0, keepdims=True)
        if extra is not None:
            tot = tot + pred(extra, extra_idx).astype(I32)
        return tot

    def smallest(pred):
        def body(start, acc):
            v = sc_scr[pl.ds(start, sbs), :]
            kept = jnp.where(pred(v, start + k_off), v, jnp.inf)
            return jnp.minimum(acc, jnp.min(kept.reshape(sbs // acc_rows, acc_rows, qb), axis=0))
        out = jnp.min(fold(body, jnp.full((acc_rows, qb), jnp.inf, F32)), axis=0, keepdims=True)
        if extra is not None:
            out = jnp.minimum(out, jnp.where(pred(extra, extra_idx), extra, jnp.inf))
        return out

    def write(select):
        def body(start, carry):
            idx = start + k_off
            sel = select(sc_scr[pl.ds(start, sbs), :], idx)
            if valid_at is not None:
                sel = sel & valid_at(idx)
            bias_scr[pl.ds(start, sbs), :] = jnp.where(sel, 0.0, NEG_BIG).astype(F32)
            return carry
        fold(body, 0)
        if extra is not None:
            extra_bias_ref[...] = jnp.where(select(extra, extra_idx), 0.0, NEG_BIG).astype(F32)

    few = count(lambda v, i: v > -jnp.inf) <= k
    def count_ge(t):
        cand = _float_of_key(t)
        return count(lambda v, i: v >= cand)

    zero = jnp.zeros((1, qb), I32)
    t0 = jnp.where(count_ge(zero) >= k, zero, jnp.full((1, qb), INT_MIN, I32))

    def bit_step(j, t):
        cand = t | jnp.left_shift(jnp.int32(1), 30 - j)
        return jnp.where(count_ge(cand) >= k, cand, t)

    thr = jnp.where(few, -jnp.inf, _float_of_key(lax.fori_loop(0, 31, bit_step, t0)))
    cnt = count(lambda v, i: v >= thr)
    over = (cnt > k) & jnp.logical_not(few)
    any_over = jnp.max(over.astype(I32)) > 0

    @pl.when(jnp.logical_not(any_over))
    def _():
        write(lambda v, i: v >= thr)

    @pl.when(any_over)
    def _():
        lo, strict, n_kept = thr, jnp.zeros((1, qb), jnp.bool_), cnt
        kept = lambda lo, strict: (lambda v, i: (v > lo) | ((v == lo) & jnp.logical_not(strict)))
        for _ in range(TOPK_TRIM_ROUNDS):
            m = smallest(kept(lo, strict))
            n_m = count(lambda v, i: v == m)
            drop = over & (n_kept - n_m >= k)
            lo = jnp.where(drop, m, lo)
            strict = strict | drop
            n_kept = jnp.where(drop, n_kept - n_m, n_kept)
        m = smallest(kept(lo, strict))
        tied = over & (n_kept > k)
        need = k - (n_kept - count(lambda v, i: v == m))
        n_bits = n_total.bit_length()

        def idx_step(j, lim):
            cand = lim | jnp.left_shift(jnp.int32(1), n_bits - 1 - j)
            below = count(lambda v, i: (v == m) & (i < cand))
            return jnp.where(below < need, cand, lim)

        last = jnp.where(tied, lax.fori_loop(0, n_bits, idx_step, jnp.zeros((1, qb), I32)), n_total)
        keep = kept(lo, strict)
        write(lambda v, i: keep(v, i) & ((v != m) | (i <= last)))


def _dsa_prompt_body(qi_ref, wit_ref, q_ref, ki_ref, k_ref, vt_ref, o_ref, sc_scr, bias_scr, m_scr, acc_scr,
                     *, topk):
    i = pl.program_id(1)
    qb = PROMPT_Q_BLOCK
    sbs = PROMPT_SCORE_BLOCK
    kbs = PROMPT_KEY_BLOCK
    n_keys = i * qb + qb
    nsb = (n_keys + sbs - 1) // sbs
    nkb = (n_keys + kbs - 1) // kbs
    q_pos = i * qb + lax.broadcasted_iota(I32, (sbs, qb), 1)
    k_off = lax.broadcasted_iota(I32, (sbs, qb), 0)
    qi_stack = jnp.concatenate(
        [qi_ref[:, h * IDX_DIM:(h + 1) * IDX_DIM] for h in range(IDX_HEADS)], axis=0)
    w_rows = wit_ref[...]

    def score_block(sb, carry):
        start = pl.multiple_of(sb * sbs, sbs)
        d = _dot_nt(ki_ref[pl.ds(start, sbs), :], qi_stack)
        acc = jnp.zeros((sbs, qb), F32)
        for h in range(IDX_HEADS):
            acc = acc + jnp.maximum(d[:, h * qb:(h + 1) * qb], 0.0) * w_rows[h:h + 1, :]
        sc = jnp.where(start + k_off <= q_pos, _positive_zero(acc), -jnp.inf)
        sc_scr[pl.ds(start, sbs), :] = sc
        return carry

    lax.fori_loop(0, nsb, score_block, 0)
    _select_topk_lanes(sc_scr, bias_scr, nsb, sbs, topk, lambda idx: idx <= q_pos)

    rep = N_HEADS // N_KV_HEADS
    m_scr[...] = jnp.full(m_scr.shape, NEG_BIG, m_scr.dtype)
    acc_scr[...] = jnp.zeros_like(acc_scr)

    groups = range(N_KV_HEADS)

    def masked_logits(kb):
        start = pl.multiple_of(kb * kbs, kbs)
        bias = jnp.concatenate([bias_scr[pl.ds(start, kbs), :]] * rep, axis=1)
        out = []
        for g in groups:
            qg = jnp.concatenate(
                [q_ref[:, (g * rep + r) * HEAD_DIM:(g * rep + r + 1) * HEAD_DIM] for r in range(rep)], axis=0)
            lg32 = _dot_nt(k_ref[pl.ds(start, kbs), g * HEAD_DIM:(g + 1) * HEAD_DIM], qg) + bias
            out.append(lg32.astype(BF16))
        return out

    def att_block(kb, carry):
        m_old = [m_scr[g] for g in groups]
        acc_old = [acc_scr[g] for g in groups]
        lg = masked_logits(kb)
        m_new = [jnp.maximum(m_old[g], jnp.max(lg[g], axis=0, keepdims=True)) for g in groups]
        p = [jnp.exp2(lg[g] - m_new[g]) for g in groups]
        pv = [_dot(vt_ref[kb, g], p[g]) for g in groups]
        for g in groups:
            alpha = jnp.exp2(m_old[g].astype(F32) - m_new[g].astype(F32))
            acc_scr[g] = alpha * acc_old[g] + pv[g]
            m_scr[g] = m_new[g]
        return carry

    lax.fori_loop(0, nkb, att_block, 0)

    for g in range(N_KV_HEADS):
        acc = acc_scr[g]
        out_t = acc[:HEAD_DIM, :] / acc[HEAD_DIM:HEAD_DIM + 1, :]
        for pr in range(rep // 2):
            pair = jnp.concatenate([out_t[:, (2 * pr) * qb:(2 * pr + 1) * qb],
                                    out_t[:, (2 * pr + 1) * qb:(2 * pr + 2) * qb]], axis=0)
            c0 = (g * rep + 2 * pr) * HEAD_DIM
            o_ref[:, c0:c0 + 2 * HEAD_DIM] = pair.T


def _dsa_prompt(qib, wi_t, qb16, kib, kb16, vt4, bsz, seq):
    topk = min(TOPK_MAX, seq // 4)
    qblk = PROMPT_Q_BLOCK
    kbs = PROMPT_KEY_BLOCK
    nq = seq // qblk
    tok = lambda n: pl.BlockSpec((qblk, n), lambda b, i: (b * nq + i, 0))
    per_seq = lambda n: pl.BlockSpec((seq, n), lambda b, i: (b, 0))
    return pl.pallas_call(
        functools.partial(_dsa_prompt_body, topk=topk),
        grid=(bsz, nq),
        in_specs=[tok(IDX_HEADS * IDX_DIM),
                  pl.BlockSpec((IDX_HEADS, qblk), lambda b, i: (0, b * nq + i)),
                  tok(ATTN_WIDTH),
                  per_seq(IDX_DIM), per_seq(KV_WIDTH),
                  pl.BlockSpec((seq // kbs, N_KV_HEADS, V_ROWS, kbs), lambda b, i: (b, 0, 0, 0))],
        out_specs=tok(ATTN_WIDTH),
        out_shape=jax.ShapeDtypeStruct((bsz * seq, ATTN_WIDTH), F32),
        scratch_shapes=[pltpu.VMEM((seq, qblk), F32), pltpu.VMEM((seq, qblk), F32),
                        pltpu.VMEM((N_KV_HEADS, 1, (N_HEADS // N_KV_HEADS) * qblk), BF16),
                        pltpu.VMEM((N_KV_HEADS, V_ROWS, (N_HEADS // N_KV_HEADS) * qblk), F32)],
        compiler_params=pltpu.CompilerParams(dimension_semantics=("arbitrary", "arbitrary"),
                                             vmem_limit_bytes=VMEM_LIMIT),
        name="dsa_prompt",
    )(qib, wi_t, qb16, kib, kb16, vt4)


def _values_feature_major(vb16, kbs):
    m = vb16.shape[0]
    v3 = vb16.reshape(m, N_KV_HEADS, HEAD_DIM)
    ones = jnp.ones((m, N_KV_HEADS, 1), BF16)
    pad = jnp.zeros((m, N_KV_HEADS, V_ROWS - HEAD_DIM - 1), BF16)
    v_aug = jnp.concatenate([v3, ones, pad], axis=2)
    return jnp.transpose(v_aug.reshape(m // kbs, kbs, N_KV_HEADS, V_ROWS), (0, 2, 3, 1))


def _merge_ffn_body(x_ref, ys_ref, ya_ref, ga_ref, gb_ref, wso_ref, wao_ref, wo_ref, gmp_ref, gfp_ref,
                    wgu_ref, wd_ref, gfo_ref, o_ref):
    mixed = (_sigmoid(ga_ref[...]) * _dot(ys_ref[...].astype(BF16), wso_ref[...])
             + _sigmoid(gb_ref[...]) * _dot(ya_ref[...].astype(BF16), wao_ref[...]))
    x1 = x_ref[...] + _rmsnorm(_dot(mixed.astype(BF16), wo_ref[...]), gmp_ref[...])
    h2 = _rmsnorm(x1, gfp_ref[...]).astype(BF16)
    gate = _dot(h2, wgu_ref[:, :D_FF])
    up = _dot(h2, wgu_ref[:, D_FF:])
    act = (_silu(gate) * up).astype(BF16)
    o_ref[...] = x1 + _rmsnorm(_dot(act, wd_ref[...]), gfo_ref[...])


def _merge_ffn(x2d, y_ssd, y_attn, ga, gb, wso, wao, wo, gmp, gfp, wgu, wd, gfo, tm):
    m = x2d.shape[0]
    row = lambda n: pl.BlockSpec((tm, n), lambda i: (i, 0))
    return pl.pallas_call(
        _merge_ffn_body,
        grid=(m // tm,),
        in_specs=[row(D_MODEL), row(D_INNER), row(ATTN_WIDTH), row(D_MODEL), row(D_MODEL),
                  _const_spec((D_INNER, D_MODEL)), _const_spec((ATTN_WIDTH, D_MODEL)),
                  _const_spec((D_MODEL, D_MODEL)), _const_spec((1, D_MODEL)), _const_spec((1, D_MODEL)),
                  _const_spec((D_MODEL, 2 * D_FF)), _const_spec((D_FF, D_MODEL)), _const_spec((1, D_MODEL))],
        out_specs=row(D_MODEL),
        out_shape=jax.ShapeDtypeStruct((m, D_MODEL), F32),
        compiler_params=pltpu.CompilerParams(dimension_semantics=("arbitrary",), vmem_limit_bytes=VMEM_LIMIT),
        name="merge_ffn",
    )(x2d, y_ssd, y_attn, ga, gb, wso, wao, wo, gmp, gfp, wgu, wd, gfo)


def _ssd_sample_body(xbc_ref, z_ref, sm_ref, sconv_ref, cw_ref, cbias_ref, dtb_ref, a_ref, dskip_ref,
                     gssd_ref, st_ref, y_ref, conv_ref, sto_ref,
                     xs_scr, bs_scr, cs_scr, xt_scr, dtt_scr, dect_scr, yt_scr):
    b = pl.program_id(0)
    nb = pl.num_programs(0)
    hd = SSD_HEAD_DIM
    gs = SSD_GROUPS * SSD_STATE

    @pl.when(b == 0)
    def _():
        xb = xbc_ref[...]
        conv = cbias_ref[...]
        for j in range(CONV_W - 1):
            conv = conv + cw_ref[j:j + 1, :] * sconv_ref[j]
        conv = conv + cw_ref[CONV_W - 1:CONV_W, :] * xb
        for j in range(CONV_W - 2):
            conv_ref[j] = sconv_ref[j + 1]
        conv_ref[CONV_W - 2] = xb
        act = _silu(conv)
        xs = act[:, :D_INNER]
        xs_scr[...] = xs
        bs_scr[...] = act[:, D_INNER:D_INNER + gs]
        cs_scr[...] = act[:, D_INNER + gs:]
        dt = _softplus(sm_ref[...] + dtb_ref[...])
        dec = jnp.exp(dt * -jnp.exp(a_ref[...]))
        dt_full = jnp.concatenate(
            [jnp.broadcast_to(dt[:, h:h + 1], (dt.shape[0], hd)) for h in range(SSD_HEADS)], axis=1)
        dec_full = jnp.concatenate(
            [jnp.broadcast_to(dec[:, h:h + 1], (dt.shape[0], hd)) for h in range(SSD_HEADS)], axis=1)
        xt_scr[...] = xs.T
        dtt_scr[...] = dt_full.T
        dect_scr[...] = dec_full.T
        yt_scr[...] = jnp.zeros_like(yt_scr)

    nbl = xt_scr.shape[1]
    lane_b = lax.broadcasted_iota(I32, (D_INNER, nbl), 1) == b

    def pick(ref):
        return jnp.sum(jnp.where(lane_b, ref[...], 0.0), axis=1, keepdims=True)

    x_col = pick(xt_scr)
    dt_col = pick(dtt_scr)
    dec_col = pick(dect_scr)
    b_row = bs_scr[pl.ds(b, 1), :]
    c_row = cs_scr[pl.ds(b, 1), :]
    rows_pg = (SSD_HEADS // SSD_GROUPS) * hd
    y_cols = []
    for g in range(SSD_GROUPS):
        r0 = g * rows_pg
        hst = st_ref[0, r0:r0 + rows_pg, :]
        bg = b_row[:, g * SSD_STATE:(g + 1) * SSD_STATE]
        cg = c_row[:, g * SSD_STATE:(g + 1) * SSD_STATE]
        hn = dec_col[r0:r0 + rows_pg] * hst + (x_col[r0:r0 + rows_pg] * bg) * dt_col[r0:r0 + rows_pg]
        sto_ref[0, r0:r0 + rows_pg, :] = hn
        y_cols.append(jnp.sum(hn * cg, axis=1, keepdims=True))
    y_col = jnp.concatenate(y_cols, axis=0)
    yt_scr[...] = jnp.where(lane_b, y_col, yt_scr[...])

    @pl.when(b == nb - 1)
    def _():
        y = yt_scr[...].T + dskip_ref[...] * xs_scr[...]
        u = y * _silu(z_ref[...])
        gw = D_INNER // SSD_GROUPS
        for g in range(SSD_GROUPS):
            ug = u[:, g * gw:(g + 1) * gw]
            y_ref[:, g * gw:(g + 1) * gw] = _rmsnorm(ug, gssd_ref[:, g * gw:(g + 1) * gw])


def _ssd_sample(xbc, z, small, sconv, conv_w, conv_b, dtb_pad, a_pad, dskip_full, g_ssd, state):
    nb = xbc.shape[0]
    full = lambda shape: pl.BlockSpec(shape, lambda b: (0,) * len(shape))
    rows = SSD_HEADS * SSD_HEAD_DIM
    st_spec = pl.BlockSpec((1, rows, SSD_STATE), lambda b: (b, 0, 0))
    return pl.pallas_call(
        _ssd_sample_body,
        grid=(nb,),
        in_specs=[full((nb, CONV_DIM)), full((nb, D_INNER)), full((nb, LANES)),
                  full((CONV_W - 1, nb, CONV_DIM)),
                  full((CONV_W, CONV_DIM)), full((1, CONV_DIM)), full((1, LANES)), full((1, LANES)),
                  full((1, D_INNER)), full((1, D_INNER)), st_spec],
        out_specs=[full((nb, D_INNER)), full((CONV_W - 1, nb, CONV_DIM)), st_spec],
        out_shape=[jax.ShapeDtypeStruct((nb, D_INNER), F32),
                   jax.ShapeDtypeStruct((CONV_W - 1, nb, CONV_DIM), F32),
                   jax.ShapeDtypeStruct((nb, rows, SSD_STATE), F32)],
        scratch_shapes=[pltpu.VMEM((nb, D_INNER), F32),
                        pltpu.VMEM((nb, SSD_GROUPS * SSD_STATE), F32),
                        pltpu.VMEM((nb, SSD_GROUPS * SSD_STATE), F32),
                        pltpu.VMEM((D_INNER, nb), F32), pltpu.VMEM((D_INNER, nb), F32),
                        pltpu.VMEM((D_INNER, nb), F32), pltpu.VMEM((D_INNER, nb), F32)],
        compiler_params=pltpu.CompilerParams(dimension_semantics=("arbitrary",), vmem_limit_bytes=VMEM_LIMIT),
        name="ssd_sample",
    )(xbc, z, small, sconv, conv_w, conv_b, dtb_pad, a_pad, dskip_full, g_ssd, state)


class _PagePipeline:
    def __init__(self, pt_ref, pairs, sem, pages, steps_per_seq):
        self.pt_ref, self.pairs, self.sem, self.pages, self.steps_per_seq = pt_ref, pairs, sem, pages, steps_per_seq

    def _copies(self, step, j):
        seq = step // self.steps_per_seq
        chunk = step % self.steps_per_seq
        slot = step % 2
        page = self.pt_ref[seq, chunk * self.pages + j]
        return [pltpu.make_async_copy(src.at[page], buf.at[slot, j], self.sem.at[slot]) for src, buf in self.pairs]

    def start(self, step):
        def body(j, carry):
            for cp in self._copies(step, j):
                cp.start()
            return carry
        lax.fori_loop(0, self.pages, body, 0)

    def wait(self, step):
        slot = step % 2
        for src, buf in self.pairs:
            pltpu.make_async_copy(src.at[pl.ds(0, self.pages)], buf.at[slot], self.sem.at[slot]).wait()

    def advance(self, step, n_steps):
        @pl.when(step == 0)
        def _():
            self.start(step)

        @pl.when(step + 1 < n_steps)
        def _():
            self.start(step + 1)

        self.wait(step)


def _idx_scores_body(pt_ref, qi_ref, wi_ref, cache_hbm, o_ref, buf, sem, *, pages):
    step = pl.program_id(0) * pl.num_programs(1) + pl.program_id(1)
    n_steps = pl.num_programs(0) * pl.num_programs(1)
    _PagePipeline(pt_ref, [(cache_hbm, buf)], sem, pages, pl.num_programs(1)).advance(step, n_steps)
    slot = step % 2
    qi = qi_ref[0]
    wi = wi_ref[0]
    sub = SAMPLE_SUB_PAGES
    for s in range(pages // sub):
        keys_t = jnp.concatenate([buf[slot, s * sub + j] for j in range(sub)], axis=1).astype(BF16)
        d = _dot(qi, keys_t)
        o_ref[0, :, s * sub * PAGE_SIZE:(s + 1) * sub * PAGE_SIZE] = jnp.sum(
            jnp.maximum(d, 0.0) * wi, axis=0, keepdims=True)


def _idx_scores(page_table, qi3, wi3, cache_idx, pages):
    nb, n_pages = page_table.shape
    nch = n_pages // pages
    span = pages * PAGE_SIZE
    grid_spec = pltpu.PrefetchScalarGridSpec(
        num_scalar_prefetch=1,
        grid=(nb, nch),
        in_specs=[pl.BlockSpec((1, IDX_HEADS, IDX_DIM), lambda b, c, pt: (b, 0, 0)),
                  pl.BlockSpec((1, IDX_HEADS, 1), lambda b, c, pt: (b, 0, 0)),
                  pl.BlockSpec(memory_space=pl.ANY)],
        out_specs=pl.BlockSpec((1, 1, span), lambda b, c, pt: (b, 0, c)),
        scratch_shapes=[pltpu.VMEM((2, pages, IDX_DIM, PAGE_SIZE), F32), pltpu.SemaphoreType.DMA((2,))],
    )
    return pl.pallas_call(
        functools.partial(_idx_scores_body, pages=pages),
        grid_spec=grid_spec,
        out_shape=jax.ShapeDtypeStruct((nb, 1, n_pages * PAGE_SIZE), F32),
        compiler_params=pltpu.CompilerParams(dimension_semantics=("arbitrary", "arbitrary"),
                                             vmem_limit_bytes=VMEM_LIMIT),
        name="idx_scores_sample",
    )(page_table, qi3, wi3, cache_idx)


def _select_sample_body(sct_ref, qi_ref, ki_ref, sm_ref, bias_ref, biasn_ref, sc_scr, *, topk):
    length, rows = sct_ref.shape
    sbs = KEY_BLOCK
    for sb in range(length // sbs):
        sc_scr[sb * sbs:(sb + 1) * sbs, :] = _positive_zero(sct_ref[sb * sbs:(sb + 1) * sbs, :])
    ki = ki_ref[...].astype(F32)
    sm = sm_ref[...]
    sc_new = jnp.zeros((rows, 1), F32)
    for h in range(IDX_HEADS):
        d = jnp.sum(qi_ref[:, h * IDX_DIM:(h + 1) * IDX_DIM].astype(F32) * ki, axis=1, keepdims=True)
        sc_new = sc_new + jnp.maximum(d, 0.0) * sm[:, SM_WI + h:SM_WI + h + 1]
    new_row = jnp.broadcast_to(_positive_zero(sc_new), (rows, rows)).T[0:1, :]
    _select_topk_lanes(sc_scr, bias_ref, length // sbs, sbs, topk, None, extra=new_row, extra_bias_ref=biasn_ref)


def _select_sample(scores_t, qib, kib, small, topk):
    length, nb = scores_t.shape
    return pl.pallas_call(
        functools.partial(_select_sample_body, topk=topk),
        out_shape=[jax.ShapeDtypeStruct((length, nb), F32), jax.ShapeDtypeStruct((1, nb), F32)],
        scratch_shapes=[pltpu.VMEM((length, nb), F32)],
        compiler_params=pltpu.CompilerParams(vmem_limit_bytes=VMEM_LIMIT),
        name="select_sample",
    )(scores_t, qib, kib, small)


def _attn_sample_body(pt_ref, q_ref, bias_ref, kn_ref, vn_ref, biasn_ref, ck_hbm, cv_hbm, o_ref,
                      kbuf, vbuf, sem, m_scr, l_scr, acc_scr, *, pages):
    c = pl.program_id(1)
    step = pl.program_id(0) * pl.num_programs(1) + c
    n_steps = pl.num_programs(0) * pl.num_programs(1)
    _PagePipeline(pt_ref, [(ck_hbm, kbuf), (cv_hbm, vbuf)], sem, pages, pl.num_programs(1)).advance(step, n_steps)
    slot = step % 2
    rep = N_HEADS // N_KV_HEADS

    @pl.when(c == 0)
    def _():
        m_scr[...] = jnp.full(m_scr.shape, NEG_BIG, F32)
        l_scr[...] = jnp.zeros_like(l_scr)
        acc_scr[...] = jnp.zeros_like(acc_scr)

    q = q_ref[0]
    head = lax.broadcasted_iota(I32, (N_HEADS, KV_WIDTH), 0)
    lane = lax.broadcasted_iota(I32, (N_HEADS, KV_WIDTH), 1)
    own = (lane // HEAD_DIM) == (head // rep)
    q_bd = jnp.where(own, jnp.concatenate([q.astype(F32)] * N_KV_HEADS, axis=1), 0.0).astype(BF16)
    sub = SAMPLE_SUB_PAGES
    for s in range(pages // sub):
        kk_t = jnp.concatenate([kbuf[slot, s * sub + j] for j in range(sub)], axis=1).astype(BF16)
        vv_t = jnp.concatenate([vbuf[slot, s * sub + j] for j in range(sub)], axis=1).astype(BF16)
        lg = _dot(q_bd, kk_t) + bias_ref[0, :, s * sub * PAGE_SIZE:(s + 1) * sub * PAGE_SIZE]
        m_old = m_scr[...]
        m_new = jnp.maximum(m_old, jnp.max(lg, axis=1, keepdims=True))
        alpha = jnp.exp2(m_old - m_new)
        p = jnp.exp2(lg - m_new)
        l_scr[...] = alpha * l_scr[...] + jnp.sum(p, axis=1, keepdims=True)
        acc_scr[...] = alpha * acc_scr[...] + _dot_nt(p.astype(BF16), vv_t)
        m_scr[...] = m_new

    @pl.when(c == pl.num_programs(1) - 1)
    def _():
        kn = kn_ref[0].astype(F32)
        lg_n = jnp.sum(q_bd.astype(F32) * kn, axis=1, keepdims=True) + biasn_ref[0][:, 0:1]
        m_o = m_scr[...]
        m_n = jnp.maximum(m_o, lg_n)
        al = jnp.exp2(m_o - m_n)
        pn = jnp.exp2(lg_n - m_n)
        l_fin = al * l_scr[...] + pn
        acc = al * acc_scr[...] + pn.astype(BF16).astype(F32) * vn_ref[0].astype(F32)
        res = jnp.where(own, acc / l_fin, 0.0)
        out = res[:, 0:HEAD_DIM]
        for g in range(1, N_KV_HEADS):
            out = out + res[:, g * HEAD_DIM:(g + 1) * HEAD_DIM]
        o_ref[0] = out


def _attn_sample(page_table, q3, bias3, kn3, vn3, biasn3, cache_k, cache_v, pages):
    nb, n_pages = page_table.shape
    nch = n_pages // pages
    span = pages * PAGE_SIZE
    per_b = lambda s: pl.BlockSpec((1,) + s, lambda b, c, pt: (b, 0, 0))
    grid_spec = pltpu.PrefetchScalarGridSpec(
        num_scalar_prefetch=1,
        grid=(nb, nch),
        in_specs=[per_b((N_HEADS, HEAD_DIM)),
                  pl.BlockSpec((1, 1, span), lambda b, c, pt: (b, 0, c)),
                  per_b((1, KV_WIDTH)), per_b((1, KV_WIDTH)), per_b((1, LANES)),
                  pl.BlockSpec(memory_space=pl.ANY), pl.BlockSpec(memory_space=pl.ANY)],
        out_specs=per_b((N_HEADS, HEAD_DIM)),
        scratch_shapes=[pltpu.VMEM((2, pages, KV_WIDTH, PAGE_SIZE), F32),
                        pltpu.VMEM((2, pages, KV_WIDTH, PAGE_SIZE), F32),
                        pltpu.SemaphoreType.DMA((2,)),
                        pltpu.VMEM((N_HEADS, 1), F32), pltpu.VMEM((N_HEADS, 1), F32),
                        pltpu.VMEM((N_HEADS, KV_WIDTH), F32)],
    )
    return pl.pallas_call(
        functools.partial(_attn_sample_body, pages=pages),
        grid_spec=grid_spec,
        out_shape=jax.ShapeDtypeStruct((nb, N_HEADS, HEAD_DIM), F32),
        compiler_params=pltpu.CompilerParams(dimension_semantics=("arbitrary", "arbitrary"),
                                             vmem_limit_bytes=VMEM_LIMIT),
        name="attn_sample",
    )(page_table, q3, bias3, kn3, vn3, biasn3, cache_k, cache_v)


def _rope_tables(pos):
    half = HEAD_DIM // 2
    inv = ROPE_THETA ** (-(jnp.arange(half, dtype=F32) * 2.0) / HEAD_DIM)
    ang = pos.astype(F32)[:, None] * inv[None, :]
    cos = jnp.cos(ang)
    sin = jnp.sin(ang)
    cos_t = jnp.concatenate([cos, cos, cos, cos], axis=1)
    sin_t = jnp.concatenate([-sin, sin, -sin, sin], axis=1)
    return cos_t, sin_t


def _permute_w_in(w):
    sizes = (D_INNER, CONV_DIM, SSD_HEADS, ATTN_WIDTH, KV_WIDTH, KV_WIDTH,
             IDX_HEADS * IDX_DIM, IDX_DIM, IDX_HEADS, D_MODEL, D_MODEL)
    cuts = np.concatenate([[0], np.cumsum(sizes)])
    z, xbc, dt, q, k, v, qi, ki, wi, ga, gb = [w[:, int(cuts[j]):int(cuts[j + 1])] for j in range(len(sizes))]
    pad = jnp.zeros((w.shape[0], SM_KI - SM_WI - IDX_HEADS), w.dtype)
    return jnp.concatenate([z, xbc, q, k, v, qi, ga, gb, dt, wi, pad, ki], axis=1).astype(BF16)


def _pad_lanes(v, offset=0):
    out = jnp.zeros((1, LANES), F32)
    return out.at[0, offset:offset + v.shape[0]].set(v.astype(F32))


def kernel(x_prompt, x_sample, cache_k, cache_v, cache_idx_k, state_ssm, state_conv, page_table, g_mix_pre, w_in,
           g_idx_k, b_idx_k, conv_w, conv_b, dt_bias, a_log, d_skip, g_ssd, w_ssd_out, w_attn_out, w_o, g_mix_post,
           g_ffn_pre, w_gate_up, w_down, g_ffn_post):
    bp, sp, _ = x_prompt.shape
    bd, ts, _ = x_sample.shape
    assert ts == 1 and w_in.shape[0] == 1, "one decode token per sample sequence, depth 1"
    n_pages = page_table.shape[1]
    past = n_pages * PAGE_SIZE
    layer = 0

    w_perm = _permute_w_in(w_in[layer])
    g_pre = g_mix_pre[layer][None, :]
    gik = _pad_lanes(g_idx_k[layer], SM_KI)
    bik = _pad_lanes(b_idx_k[layer], SM_KI)
    dtb = _pad_lanes(dt_bias[layer])
    a_pad = _pad_lanes(a_log[layer])
    dskip_full = jnp.repeat(d_skip[layer].astype(F32), SSD_HEAD_DIM)[None, :]
    gssd = g_ssd[layer][None, :]
    cw = conv_w[layer]
    cbias = conv_b[layer][None, :]
    wso = w_ssd_out[layer].astype(BF16)
    wao = w_attn_out[layer].astype(BF16)
    wo = w_o[layer].astype(BF16)
    wgu = w_gate_up[layer].astype(BF16)
    wd = w_down[layer].astype(BF16)
    gmp = g_mix_post[layer][None, :]
    gfp = g_ffn_pre[layer][None, :]
    gfo = g_ffn_post[layer][None, :]

    tm = 256
    xp2 = x_prompt.reshape(bp * sp, D_MODEL)
    cos_p, sin_p = _rope_tables(jnp.arange(sp))
    tiles_per_seq = sp // tm
    (z, xbc, k, v, ga, gb, ki, small, qb16, kb16, vb16, qib, kib) = _in_projection(
        xp2, cos_p, sin_p, lambda i: (i % tiles_per_seq, 0), g_pre, w_perm, gik, bik, tm)
    y_ssd, ssm_p, tail = _ssd_prompt(xbc, z, small, cw, cbias, dtb, a_pad, dskip_full, gssd, bp, sp)
    wi_t = small[:, SM_WI:SM_WI + IDX_HEADS].T
    y_attn = _dsa_prompt(qib, wi_t, qb16, kib, kb16, _values_feature_major(vb16, PROMPT_KEY_BLOCK), bp, sp)
    yp = _merge_ffn(xp2, y_ssd, y_attn, ga, gb, wso, wao, wo, gmp, gfp, wgu, wd, gfo, tm)

    y_prompt = yp.reshape(bp, sp, D_MODEL)
    k_prompt = k.reshape(1, bp, sp, N_KV_HEADS, HEAD_DIM)
    v_prompt = v.reshape(1, bp, sp, N_KV_HEADS, HEAD_DIM)
    idx_k_prompt = ki.reshape(1, bp, sp, IDX_DIM)
    ssm_prompt = ssm_p[None]
    conv_prompt = tail[None, :, SUBLANES - (CONV_W - 1):, :]

    xs2 = x_sample.reshape(bd, D_MODEL)
    cos_s, sin_s = _rope_tables(jnp.full((bd,), past, jnp.int32))
    (z, xbc, k, v, ga, gb, ki, small, qb16, kb16, vb16, qib, kib) = _in_projection(
        xs2, cos_s, sin_s, lambda i: (i, 0), g_pre, w_perm, gik, bik, bd)
    sconv = jnp.transpose(state_conv[layer], (1, 0, 2))
    st_in = state_ssm[layer].reshape(bd, SSD_HEADS * SSD_HEAD_DIM, SSD_STATE)
    y_ssd, conv_s, st_out = _ssd_sample(xbc, z, small, sconv, cw, cbias, dtb, a_pad, dskip_full, gssd, st_in)

    assert n_pages % SAMPLE_SUB_PAGES == 0, "page count must be a multiple of the per-matmul page group"
    idx_pages = math.gcd(IDX_PAGES_PER_STEP, n_pages)
    attn_pages = math.gcd(ATTN_PAGES_PER_STEP, n_pages)
    topk = min(TOPK_MAX, (past + ts) // 4)
    wi3 = small[:, SM_WI:SM_WI + IDX_HEADS].reshape(bd, IDX_HEADS, 1)
    cidx = jnp.transpose(cache_idx_k[layer], (0, 2, 1))
    ck = jnp.transpose(cache_k[layer], (0, 2, 3, 1)).reshape(-1, KV_WIDTH, PAGE_SIZE)
    cv = jnp.transpose(cache_v[layer], (0, 2, 3, 1)).reshape(-1, KV_WIDTH, PAGE_SIZE)
    scores = _idx_scores(page_table, qib.reshape(bd, IDX_HEADS, IDX_DIM), wi3, cidx, idx_pages)
    bias_t, bias_new = _select_sample(scores.reshape(bd, past).T, qib, kib, small, topk)
    bias_new3 = jnp.broadcast_to(bias_new.reshape(bd, 1, 1), (bd, 1, LANES))
    y_attn = _attn_sample(page_table, qb16.reshape(bd, N_HEADS, HEAD_DIM), bias_t.T.reshape(bd, 1, past),
                          kb16.reshape(bd, 1, KV_WIDTH), vb16.reshape(bd, 1, KV_WIDTH),
                          bias_new3, ck, cv, attn_pages)
    ys = _merge_ffn(xs2, y_ssd, y_attn.reshape(bd, ATTN_WIDTH), ga, gb, wso, wao, wo, gmp, gfp, wgu, wd, gfo, bd)

    y_sample = ys.reshape(bd, ts, D_MODEL)
    k_sample = k.reshape(1, bd, ts, N_KV_HEADS, HEAD_DIM)
    v_sample = v.reshape(1, bd, ts, N_KV_HEADS, HEAD_DIM)
    idx_k_sample = ki.reshape(1, bd, ts, IDX_DIM)
    ssm_sample = st_out.reshape(1, bd, SSD_HEADS, SSD_HEAD_DIM, SSD_STATE)
    conv_sample = jnp.transpose(conv_s, (1, 0, 2))[None]
    return (y_prompt, y_sample, k_prompt, v_prompt, idx_k_prompt, ssm_prompt, conv_prompt,
            k_sample, v_sample, idx_k_sample, ssm_sample, conv_sample)
```

```python
import functools
import math

import numpy as np
import jax
import jax.numpy as jnp
from jax import lax
from jax.experimental import pallas as pl
from jax.experimental.pallas import tpu as pltpu

F32 = jnp.float32
BF16 = jnp.bfloat16
I32 = jnp.int32

D_MODEL = 1024
D_INNER = 2048
SSD_HEAD_DIM = 64
SSD_HEADS = 32
SSD_GROUPS = 4
SSD_STATE = 128
CONV_W = 4
CONV_DIM = D_INNER + 2 * SSD_GROUPS * SSD_STATE
SSD_CHUNK = 128
N_HEADS = 16
N_KV_HEADS = 4
HEAD_DIM = 64
ATTN_WIDTH = N_HEADS * HEAD_DIM
KV_WIDTH = N_KV_HEADS * HEAD_DIM
IDX_HEADS = 8
IDX_DIM = 64
IDX_SCALE = (IDX_HEADS ** -0.5) * (IDX_DIM ** -0.5)
TOPK_MAX = 256
ROPE_THETA = 10000.0
PAGE_SIZE = 128
D_FF = 2816
EPS = 1e-6

LANES = 128
SUBLANES = 8
VMEM_LIMIT = 60 * 1024 * 1024

Z0 = 0
XBC0 = Z0 + D_INNER
Q0 = XBC0 + CONV_DIM
K0 = Q0 + ATTN_WIDTH
V0 = K0 + KV_WIDTH
QI0 = V0 + KV_WIDTH
GA0 = QI0 + IDX_HEADS * IDX_DIM
GB0 = GA0 + D_MODEL
SM0 = GB0 + D_MODEL
N_PROJ = SM0 + LANES
SM_DT = 0
SM_WI = SSD_HEADS
SM_KI = 64

Q_SCALE_LOG2 = (HEAD_DIM ** -0.5) * math.log2(math.e)
NEG_BIG = -1e30
INT_MIN = -(2 ** 31)
F32_MIN_NORMAL_BITS = 0x00800000
TOPK_TRIM_ROUNDS = 2
KEY_BLOCK = 512
PROMPT_Q_BLOCK = 4 * LANES
PROMPT_SCORE_BLOCK = 512
PROMPT_KEY_BLOCK = 512
V_ROWS = 80
IDX_PAGES_PER_STEP = 64
ATTN_PAGES_PER_STEP = 32
SAMPLE_SUB_PAGES = 16


def _dot(a, b):
    return jnp.dot(a, b, preferred_element_type=F32)


def _dot_nt(a, b):
    return lax.dot_general(a, b, (((1,), (1,)), ((), ())), preferred_element_type=F32)


def _sigmoid(x):
    return 1.0 / (1.0 + jnp.exp(-x))


def _silu(x):
    return x * _sigmoid(x)


def _softplus(x):
    return jnp.maximum(x, 0.0) + jnp.log1p(jnp.exp(-jnp.abs(x)))


def _rmsnorm(x, g):
    return x * lax.rsqrt(jnp.mean(x * x, axis=-1, keepdims=True) + EPS) * g


def _rope_tile(x, cos, sin_signed, first_half):
    partner = jnp.where(first_half, pltpu.roll(x, LANES - 32, 1), pltpu.roll(x, 32, 1))
    return x * cos + partner * sin_signed


def _rope_wide(x, cos, sin_signed, first_half):
    n = x.shape[1] // LANES
    return jnp.concatenate(
        [_rope_tile(x[:, c * LANES:(c + 1) * LANES], cos, sin_signed, first_half) for c in range(n)], axis=1)


def _positive_zero(x):
    return jnp.where(x == 0.0, 0.0, x)


def _const_spec(shape):
    nd = len(shape)
    return pl.BlockSpec(shape, lambda *_: (0,) * nd, pipeline_mode=pl.Buffered(1))


def _inproj_body(x_ref, cos_ref, sin_ref, g_ref, w_ref, gik_ref, bik_ref,
                 z_ref, xbc_ref, k_ref, v_ref, ga_ref, gb_ref, ki_ref, sm_ref,
                 qb_ref, kb_ref, vb_ref, qib_ref, kib_ref):
    h = _rmsnorm(x_ref[...], g_ref[...]).astype(BF16)
    cos = cos_ref[...]
    sin = sin_ref[...]
    lane = lax.broadcasted_iota(I32, cos.shape, 1)
    first_half = (lane % HEAD_DIM) < (HEAD_DIM // 2)

    def seg(off, n):
        return _dot(h, w_ref[:, off:off + n])

    z_ref[...] = seg(Z0, D_INNER)
    xbc_ref[...] = seg(XBC0, CONV_DIM)
    ga_ref[...] = seg(GA0, D_MODEL)
    gb_ref[...] = seg(GB0, D_MODEL)
    q = _rope_wide(seg(Q0, ATTN_WIDTH), cos, sin, first_half)
    qb_ref[...] = (q * Q_SCALE_LOG2).astype(BF16)
    k = _rope_wide(seg(K0, KV_WIDTH), cos, sin, first_half)
    k_ref[...] = k
    kb_ref[...] = k.astype(BF16)
    v = seg(V0, KV_WIDTH)
    v_ref[...] = v
    vb_ref[...] = v.astype(BF16)
    qib_ref[...] = _rope_wide(seg(QI0, IDX_HEADS * IDX_DIM), cos, sin, first_half).astype(BF16)
    small = seg(SM0, LANES)
    is_ki = lane >= SM_KI
    mu = jnp.sum(jnp.where(is_ki, small, 0.0), axis=-1, keepdims=True) * (1.0 / IDX_DIM)
    dev = jnp.where(is_ki, small - mu, 0.0)
    var = jnp.sum(dev * dev, axis=-1, keepdims=True) * (1.0 / IDX_DIM)
    kin = dev * lax.rsqrt(var + EPS) * gik_ref[...] + bik_ref[...]
    kir = _rope_tile(kin, cos, sin, first_half)
    ki_ref[...] = kir[:, SM_KI:]
    kib_ref[...] = kir[:, SM_KI:].astype(BF16)
    is_wi = (lane >= SM_WI) & (lane < SM_WI + IDX_HEADS)
    sm_ref[...] = jnp.where(is_wi, small * IDX_SCALE, small)


def _in_projection(x2d, cos_tab, sin_tab, tab_index, g, w_perm, gik, bik, tm):
    m = x2d.shape[0]
    row = lambda n: pl.BlockSpec((tm, n), lambda i: (i, 0))
    out_shapes = [
        jax.ShapeDtypeStruct((m, D_INNER), F32),
        jax.ShapeDtypeStruct((m, CONV_DIM), F32),
        jax.ShapeDtypeStruct((m, KV_WIDTH), F32),
        jax.ShapeDtypeStruct((m, KV_WIDTH), F32),
        jax.ShapeDtypeStruct((m, D_MODEL), F32),
        jax.ShapeDtypeStruct((m, D_MODEL), F32),
        jax.ShapeDtypeStruct((m, IDX_DIM), F32),
        jax.ShapeDtypeStruct((m, LANES), F32),
        jax.ShapeDtypeStruct((m, ATTN_WIDTH), BF16),
        jax.ShapeDtypeStruct((m, KV_WIDTH), BF16),
        jax.ShapeDtypeStruct((m, KV_WIDTH), BF16),
        jax.ShapeDtypeStruct((m, IDX_HEADS * IDX_DIM), BF16),
        jax.ShapeDtypeStruct((m, IDX_DIM), BF16),
    ]
    return pl.pallas_call(
        _inproj_body,
        grid=(m // tm,),
        in_specs=[
            row(D_MODEL),
            pl.BlockSpec((tm, LANES), tab_index),
            pl.BlockSpec((tm, LANES), tab_index),
            _const_spec((1, D_MODEL)),
            _const_spec((D_MODEL, N_PROJ)),
            _const_spec((1, LANES)),
            _const_spec((1, LANES)),
        ],
        out_specs=[row(s.shape[1]) for s in out_shapes],
        out_shape=out_shapes,
        compiler_params=pltpu.CompilerParams(dimension_semantics=("arbitrary",), vmem_limit_bytes=VMEM_LIMIT),
        name="in_projection",
    )(x2d, cos_tab, sin_tab, g, w_perm, gik, bik)


def _pair_cols(col_a, col_b, first):
    return jnp.where(first, col_a, col_b)


def _ssd_prompt_body(xbc_ref, z_ref, sm_ref, cw_ref, cbias_ref, dtb_ref, a_ref, dskip_ref, gssd_ref,
                     y_ref, ssm_ref, tail_ref, xp_scr, st_scr, y_scr):
    c = pl.program_id(1)
    q = SSD_CHUNK

    @pl.when(c == 0)
    def _():
        xp_scr[0:SUBLANES, :] = jnp.zeros((SUBLANES, CONV_DIM), F32)
        st_scr[...] = jnp.zeros_like(st_scr)

    xb = xbc_ref[...]
    xp_scr[SUBLANES:SUBLANES + q, :] = xb
    conv = cbias_ref[...]
    for j in range(CONV_W - 1):
        lo = SUBLANES - (CONV_W - 1) + j
        conv = conv + cw_ref[j:j + 1, :] * xp_scr[lo:lo + q, :]
    conv = conv + cw_ref[CONV_W - 1:CONV_W, :] * xb
    xp_scr[SUBLANES - (CONV_W - 1):SUBLANES, :] = xb[q - (CONV_W - 1):q, :]
    tail_ref[0] = xb[q - SUBLANES:q, :]

    act = _silu(conv)
    xs = act[:, :D_INNER]
    gs = SSD_GROUPS * SSD_STATE
    bs = act[:, D_INNER:D_INNER + gs]
    cs = act[:, D_INNER + gs:]

    lane = lax.broadcasted_iota(I32, (q, LANES), 1)
    rowi = lax.broadcasted_iota(I32, (q, LANES), 0)
    causal = rowi >= lane
    first = lane < SSD_HEAD_DIM
    dt = jnp.where(lane < SSD_HEADS, _softplus(sm_ref[...] + dtb_ref[...]), 0.0)
    dta = dt * -jnp.exp(a_ref[...])
    tril = jnp.where(causal, 1.0, 0.0).astype(F32)
    cum = jnp.dot(tril, dta, preferred_element_type=F32, precision=lax.Precision.HIGHEST)
    cum_t = cum.T
    cum_last = cum[q - 1:q, :]
    e_cum = jnp.exp(cum)
    d_last = jnp.exp(cum_last - cum)
    e_last = jnp.exp(cum_last)

    rpg = SSD_HEADS // SSD_GROUPS
    for g in range(SSD_GROUPS):
        cs_g = cs[:, g * SSD_STATE:(g + 1) * SSD_STATE].astype(BF16)
        bs_g = bs[:, g * SSD_STATE:(g + 1) * SSD_STATE]
        cb = _dot_nt(cs_g, bs_g.astype(BF16))
        bs_t = bs_g.T.astype(BF16)
        for pr in range(rpg // 2):
            h0 = g * rpg + 2 * pr
            h1 = h0 + 1
            hp = h0 // 2
            xs_p = xs[:, hp * LANES:(hp + 1) * LANES]
            xdt = xs_p * _pair_cols(dt[:, h0:h0 + 1], dt[:, h1:h1 + 1], first)
            xdt_b = xdt.astype(BF16)
            yd = []
            for h in (h0, h1):
                seg = jnp.where(causal, cum[:, h:h + 1] - cum_t[h:h + 1, :], -jnp.inf)
                m = (cb * jnp.exp(seg)).astype(BF16)
                yd.append(_dot(m, xdt_b))
            st = st_scr[hp]
            y_off = _dot(cs_g, st.astype(BF16)) * _pair_cols(e_cum[:, h0:h0 + 1], e_cum[:, h1:h1 + 1], first)
            y_scr[:, hp * LANES:(hp + 1) * LANES] = jnp.where(first, yd[0], yd[1]) + y_off
            xdl = xdt * _pair_cols(d_last[:, h0:h0 + 1], d_last[:, h1:h1 + 1], first)
            dec = _pair_cols(e_last[:, h0:h0 + 1], e_last[:, h1:h1 + 1], first[0:1, :])
            st_scr[hp] = dec * st + _dot(bs_t, xdl.astype(BF16))

    y = y_scr[...] + dskip_ref[...] * xs
    u = y * _silu(z_ref[...])
    gw = D_INNER // SSD_GROUPS
    for g in range(SSD_GROUPS):
        ug = u[:, g * gw:(g + 1) * gw]
        y_ref[:, g * gw:(g + 1) * gw] = _rmsnorm(ug, gssd_ref[:, g * gw:(g + 1) * gw])

    @pl.when(c == pl.num_programs(1) - 1)
    def _():
        for hp in range(SSD_HEADS // 2):
            st_t = st_scr[hp].T
            ssm_ref[0, 2 * hp] = st_t[:SSD_HEAD_DIM, :]
            ssm_ref[0, 2 * hp + 1] = st_t[SSD_HEAD_DIM:, :]


def _ssd_prompt(xbc, z, small, conv_w, conv_b, dtb_pad, a_pad, dskip_full, g_ssd, bsz, seq):
    nc = seq // SSD_CHUNK
    q = SSD_CHUNK
    tok = lambda n: pl.BlockSpec((q, n), lambda b, c: (b * nc + c, 0))
    return pl.pallas_call(
        _ssd_prompt_body,
        grid=(bsz, nc),
        in_specs=[
            tok(CONV_DIM), tok(D_INNER), tok(LANES),
            _const_spec((CONV_W, CONV_DIM)), _const_spec((1, CONV_DIM)),
            _const_spec((1, LANES)), _const_spec((1, LANES)),
            _const_spec((1, D_INNER)), _const_spec((1, D_INNER)),
        ],
        out_specs=[
            tok(D_INNER),
            pl.BlockSpec((1, SSD_HEADS, SSD_HEAD_DIM, SSD_STATE), lambda b, c: (b, 0, 0, 0)),
            pl.BlockSpec((1, SUBLANES, CONV_DIM), lambda b, c: (b, 0, 0)),
        ],
        out_shape=[
            jax.ShapeDtypeStruct((bsz * seq, D_INNER), F32),
            jax.ShapeDtypeStruct((bsz, SSD_HEADS, SSD_HEAD_DIM, SSD_STATE), F32),
            jax.ShapeDtypeStruct((bsz, SUBLANES, CONV_DIM), F32),
        ],
        scratch_shapes=[
            pltpu.VMEM((SUBLANES + q, CONV_DIM), F32),
            pltpu.VMEM((SSD_HEADS // 2, SSD_STATE, LANES), F32),
            pltpu.VMEM((q, D_INNER), F32),
        ],
        compiler_params=pltpu.CompilerParams(dimension_semantics=("arbitrary", "arbitrary"),
                                             vmem_limit_bytes=VMEM_LIMIT),
        name="ssd_prompt",
    )(xbc, z, small, conv_w, conv_b, dtb_pad, a_pad, dskip_full, g_ssd)


def _float_of_key(t):
    bits = t ^ ((t >> 31) & 0x7FFFFFFF)
    bits = jnp.where((t >= 1) & (t < F32_MIN_NORMAL_BITS), F32_MIN_NORMAL_BITS, bits)
    return pltpu.bitcast(bits, F32)


def _select_topk_lanes(sc_scr, bias_scr, nsb, sbs, k, valid_at, extra=None, extra_bias_ref=None):
    n_total, qb = sc_scr.shape
    acc_rows = 4 * SUBLANES
    k_off = lax.broadcasted_iota(I32, (sbs, qb), 0)
    extra_idx = jnp.full((1, qb), n_total, I32)

    def fold(fn, init):
        return lax.fori_loop(0, nsb, lambda sb, c: fn(pl.multiple_of(sb * sbs, sbs), c), init)

    def count(pred):
        def body(start, acc):
            hit = pred(sc_scr[pl.ds(start, sbs), :], start + k_off).astype(I32)
            return acc + jnp.sum(hit.reshape(sbs // acc_rows, acc_rows, qb), axis=0)
        tot = jnp.sum(fold(body, jnp.zeros((acc_rows, qb), I32)), axis=0, keepdims=True)
        if extra is not None:
            tot = tot + pred(extra, extra_idx).astype(I32)
        return tot

    def smallest(pred):
        def body(start, acc):
            v = sc_scr[pl.ds(start, sbs), :]
            kept = jnp.where(pred(v, start + k_off), v, jnp.inf)
            return jnp.minimum(acc, jnp.min(kept.reshape(sbs // acc_rows, acc_rows, qb), axis=0))
        out = jnp.min(fold(body, jnp.full((acc_rows, qb), jnp.inf, F32)), axis=0, keepdims=True)
        if extra is not None:
            out = jnp.minimum(out, jnp.where(pred(extra, extra_idx), extra, jnp.inf))
        return out

    def write(select):
        def body(start, carry):
            idx = start + k_off
            sel = select(sc_scr[pl.ds(start, sbs), :], idx)
            if valid_at is not None:
                sel = sel & valid_at(idx)
            bias_scr[pl.ds(start, sbs), :] = jnp.where(sel, 0.0, NEG_BIG).astype(F32)
            return carry
        fold(body, 0)
        if extra is not None:
            extra_bias_ref[...] = jnp.where(select(extra, extra_idx), 0.0, NEG_BIG).astype(F32)

    few = count(lambda v, i: v > -jnp.inf) <= k
    def count_ge(t):
        cand = _float_of_key(t)
        return count(lambda v, i: v >= cand)

    zero = jnp.zeros((1, qb), I32)
    t0 = jnp.where(count_ge(zero) >= k, zero, jnp.full((1, qb), INT_MIN, I32))

    def bit_step(j, t):
        cand = t | jnp.left_shift(jnp.int32(1), 30 - j)
        return jnp.where(count_ge(cand) >= k, cand, t)

    thr = jnp.where(few, -jnp.inf, _float_of_key(lax.fori_loop(0, 31, bit_step, t0)))
    cnt = count(lambda v, i: v >= thr)
    over = (cnt > k) & jnp.logical_not(few)
    any_over = jnp.max(over.astype(I32)) > 0

    @pl.when(jnp.logical_not(any_over))
    def _():
        write(lambda v, i: v >= thr)

    @pl.when(any_over)
    def _():
        lo, strict, n_kept = thr, jnp.zeros((1, qb), jnp.bool_), cnt
        kept = lambda lo, strict: (lambda v, i: (v > lo) | ((v == lo) & jnp.logical_not(strict)))
        for _ in range(TOPK_TRIM_ROUNDS):
            m = smallest(kept(lo, strict))
            n_m = count(lambda v, i: v == m)
            drop = over & (n_kept - n_m >= k)
            lo = jnp.where(drop, m, lo)
            strict = strict | drop
            n_kept = jnp.where(drop, n_kept - n_m, n_kept)
        m = smallest(kept(lo, strict))
        tied = over & (n_kept > k)
        need = k - (n_kept - count(lambda v, i: v == m))
        n_bits = n_total.bit_length()

        def idx_step(j, lim):
            cand = lim | jnp.left_shift(jnp.int32(1), n_bits - 1 - j)
            below = count(lambda v, i: (v == m) & (i < cand))
            return jnp.where(below < need, cand, lim)

        last = jnp.where(tied, lax.fori_loop(0, n_bits, idx_step, jnp.zeros((1, qb), I32)), n_total)
        keep = kept(lo, strict)
        write(lambda v, i: keep(v, i) & ((v != m) | (i <= last)))


def _dsa_prompt_body(qi_ref, wit_ref, q_ref, ki_ref, k_ref, vt_ref, o_ref, sc_scr, bias_scr, m_scr, acc_scr,
                     *, topk):
    i = pl.program_id(1)
    qb = PROMPT_Q_BLOCK
    sbs = PROMPT_SCORE_BLOCK
    kbs = PROMPT_KEY_BLOCK
    n_keys = i * qb + qb
    nsb = (n_keys + sbs - 1) // sbs
    nkb = (n_keys + kbs - 1) // kbs
    q_pos = i * qb + lax.broadcasted_iota(I32, (sbs, qb), 1)
    k_off = lax.broadcasted_iota(I32, (sbs, qb), 0)
    qi_stack = jnp.concatenate(
        [qi_ref[:, h * IDX_DIM:(h + 1) * IDX_DIM] for h in range(IDX_HEADS)], axis=0)
    w_rows = wit_ref[...]

    def score_block(sb, carry):
        start = pl.multiple_of(sb * sbs, sbs)
        d = _dot_nt(ki_ref[pl.ds(start, sbs), :], qi_stack)
        acc = jnp.zeros((sbs, qb), F32)
        for h in range(IDX_HEADS):
            acc = acc + jnp.maximum(d[:, h * qb:(h + 1) * qb], 0.0) * w_rows[h:h + 1, :]
        sc = jnp.where(start + k_off <= q_pos, _positive_zero(acc), -jnp.inf)
        sc_scr[pl.ds(start, sbs), :] = sc
        return carry

    lax.fori_loop(0, nsb, score_block, 0)
    _select_topk_lanes(sc_scr, bias_scr, nsb, sbs, topk, lambda idx: idx <= q_pos)

    rep = N_HEADS // N_KV_HEADS
    m_scr[...] = jnp.full(m_scr.shape, NEG_BIG, m_scr.dtype)
    acc_scr[...] = jnp.zeros_like(acc_scr)

    groups = range(N_KV_HEADS)

    def masked_logits(kb):
        start = pl.multiple_of(kb * kbs, kbs)
        bias = jnp.concatenate([bias_scr[pl.ds(start, kbs), :]] * rep, axis=1)
        out = []
        for g in groups:
            qg = jnp.concatenate(
                [q_ref[:, (g * rep + r) * HEAD_DIM:(g * rep + r + 1) * HEAD_DIM] for r in range(rep)], axis=0)
            lg32 = _dot_nt(k_ref[pl.ds(start, kbs), g * HEAD_DIM:(g + 1) * HEAD_DIM], qg) + bias
            out.append(lg32.astype(BF16))
        return out

    def att_block(kb, carry):
        m_old = [m_scr[g] for g in groups]
        acc_old = [acc_scr[g] for g in groups]
        lg = masked_logits(kb)
        m_new = [jnp.maximum(m_old[g], jnp.max(lg[g], axis=0, keepdims=True)) for g in groups]
        p = [jnp.exp2(lg[g] - m_new[g]) for g in groups]
        pv = [_dot(vt_ref[kb, g], p[g]) for g in groups]
        for g in groups:
            alpha = jnp.exp2(m_old[g].astype(F32) - m_new[g].astype(F32))
            acc_scr[g] = alpha * acc_old[g] + pv[g]
            m_scr[g] = m_new[g]
        return carry

    lax.fori_loop(0, nkb, att_block, 0)

    for g in range(N_KV_HEADS):
        acc = acc_scr[g]
        out_t = acc[:HEAD_DIM, :] / acc[HEAD_DIM:HEAD_DIM + 1, :]
        for pr in range(rep // 2):
            pair = jnp.concatenate([out_t[:, (2 * pr) * qb:(2 * pr + 1) * qb],
                                    out_t[:, (2 * pr + 1) * qb:(2 * pr + 2) * qb]], axis=0)
            c0 = (g * rep + 2 * pr) * HEAD_DIM
            o_ref[:, c0:c0 + 2 * HEAD_DIM] = pair.T


def _dsa_prompt(qib, wi_t, qb16, kib, kb16, vt4, bsz, seq):
    topk = min(TOPK_MAX, seq // 4)
    qblk = PROMPT_Q_BLOCK
    kbs = PROMPT_KEY_BLOCK
    nq = seq // qblk
    tok = lambda n: pl.BlockSpec((qblk, n), lambda b, i: (b * nq + i, 0))
    per_seq = lambda n: pl.BlockSpec((seq, n), lambda b, i: (b, 0))
    return pl.pallas_call(
        functools.partial(_dsa_prompt_body, topk=topk),
        grid=(bsz, nq),
        in_specs=[tok(IDX_HEADS * IDX_DIM),
                  pl.BlockSpec((IDX_HEADS, qblk), lambda b, i: (0, b * nq + i)),
                  tok(ATTN_WIDTH),
                  per_seq(IDX_DIM), per_seq(KV_WIDTH),
                  pl.BlockSpec((seq // kbs, N_KV_HEADS, V_ROWS, kbs), lambda b, i: (b, 0, 0, 0))],
        out_specs=tok(ATTN_WIDTH),
        out_shape=jax.ShapeDtypeStruct((bsz * seq, ATTN_WIDTH), F32),
        scratch_shapes=[pltpu.VMEM((seq, qblk), F32), pltpu.VMEM((seq, qblk), F32),
                        pltpu.VMEM((N_KV_HEADS, 1, (N_HEADS // N_KV_HEADS) * qblk), BF16),
                        pltpu.VMEM((N_KV_HEADS, V_ROWS, (N_HEADS // N_KV_HEADS) * qblk), F32)],
        compiler_params=pltpu.CompilerParams(dimension_semantics=("arbitrary", "arbitrary"),
                                             vmem_limit_bytes=VMEM_LIMIT),
        name="dsa_prompt",
    )(qib, wi_t, qb16, kib, kb16, vt4)


def _values_feature_major(vb16, kbs):
    m = vb16.shape[0]
    v3 = vb16.reshape(m, N_KV_HEADS, HEAD_DIM)
    ones = jnp.ones((m, N_KV_HEADS, 1), BF16)
    pad = jnp.zeros((m, N_KV_HEADS, V_ROWS - HEAD_DIM - 1), BF16)
    v_aug = jnp.concatenate([v3, ones, pad], axis=2)
    return jnp.transpose(v_aug.reshape(m // kbs, kbs, N_KV_HEADS, V_ROWS), (0, 2, 3, 1))


def _merge_ffn_body(x_ref, ys_ref, ya_ref, ga_ref, gb_ref, wso_ref, wao_ref, wo_ref, gmp_ref, gfp_ref,
                    wgu_ref, wd_ref, gfo_ref, o_ref):
    mixed = (_sigmoid(ga_ref[...]) * _dot(ys_ref[...].astype(BF16), wso_ref[...])
             + _sigmoid(gb_ref[...]) * _dot(ya_ref[...].astype(BF16), wao_ref[...]))
    x1 = x_ref[...] + _rmsnorm(_dot(mixed.astype(BF16), wo_ref[...]), gmp_ref[...])
    h2 = _rmsnorm(x1, gfp_ref[...]).astype(BF16)
    gate = _dot(h2, wgu_ref[:, :D_FF])
    up = _dot(h2, wgu_ref[:, D_FF:])
    act = (_silu(gate) * up).astype(BF16)
    o_ref[...] = x1 + _rmsnorm(_dot(act, wd_ref[...]), gfo_ref[...])


def _merge_ffn(x2d, y_ssd, y_attn, ga, gb, wso, wao, wo, gmp, gfp, wgu, wd, gfo, tm):
    m = x2d.shape[0]
    row = lambda n: pl.BlockSpec((tm, n), lambda i: (i, 0))
    return pl.pallas_call(
        _merge_ffn_body,
        grid=(m // tm,),
        in_specs=[row(D_MODEL), row(D_INNER), row(ATTN_WIDTH), row(D_MODEL), row(D_MODEL),
                  _const_spec((D_INNER, D_MODEL)), _const_spec((ATTN_WIDTH, D_MODEL)),
                  _const_spec((D_MODEL, D_MODEL)), _const_spec((1, D_MODEL)), _const_spec((1, D_MODEL)),
                  _const_spec((D_MODEL, 2 * D_FF)), _const_spec((D_FF, D_MODEL)), _const_spec((1, D_MODEL))],
        out_specs=row(D_MODEL),
        out_shape=jax.ShapeDtypeStruct((m, D_MODEL), F32),
        compiler_params=pltpu.CompilerParams(dimension_semantics=("arbitrary",), vmem_limit_bytes=VMEM_LIMIT),
        name="merge_ffn",
    )(x2d, y_ssd, y_attn, ga, gb, wso, wao, wo, gmp, gfp, wgu, wd, gfo)


def _ssd_sample_body(xbc_ref, z_ref, sm_ref, sconv_ref, cw_ref, cbias_ref, dtb_ref, a_ref, dskip_ref,
                     gssd_ref, st_ref, y_ref, conv_ref, sto_ref,
                     xs_scr, bs_scr, cs_scr, xt_scr, dtt_scr, dect_scr, yt_scr):
    b = pl.program_id(0)
    nb = pl.num_programs(0)
    hd = SSD_HEAD_DIM
    gs = SSD_GROUPS * SSD_STATE

    @pl.when(b == 0)
    def _():
        xb = xbc_ref[...]
        conv = cbias_ref[...]
        for j in range(CONV_W - 1):
            conv = conv + cw_ref[j:j + 1, :] * sconv_ref[j]
        conv = conv + cw_ref[CONV_W - 1:CONV_W, :] * xb
        for j in range(CONV_W - 2):
            conv_ref[j] = sconv_ref[j + 1]
        conv_ref[CONV_W - 2] = xb
        act = _silu(conv)
        xs = act[:, :D_INNER]
        xs_scr[...] = xs
        bs_scr[...] = act[:, D_INNER:D_INNER + gs]
        cs_scr[...] = act[:, D_INNER + gs:]
        dt = _softplus(sm_ref[...] + dtb_ref[...])
        dec = jnp.exp(dt * -jnp.exp(a_ref[...]))
        dt_full = jnp.concatenate(
            [jnp.broadcast_to(dt[:, h:h + 1], (dt.shape[0], hd)) for h in range(SSD_HEADS)], axis=1)
        dec_full = jnp.concatenate(
            [jnp.broadcast_to(dec[:, h:h + 1], (dt.shape[0], hd)) for h in range(SSD_HEADS)], axis=1)
        xt_scr[...] = xs.T
        dtt_scr[...] = dt_full.T
        dect_scr[...] = dec_full.T
        yt_scr[...] = jnp.zeros_like(yt_scr)

    nbl = xt_scr.shape[1]
    lane_b = lax.broadcasted_iota(I32, (D_INNER, nbl), 1) == b

    def pick(ref):
        return jnp.sum(jnp.where(lane_b, ref[...], 0.0), axis=1, keepdims=True)

    x_col = pick(xt_scr)
    dt_col = pick(dtt_scr)
    dec_col = pick(dect_scr)
    b_row = bs_scr[pl.ds(b, 1), :]
    c_row = cs_scr[pl.ds(b, 1), :]
    rows_pg = (SSD_HEADS // SSD_GROUPS) * hd
    y_cols = []
    for g in range(SSD_GROUPS):
        r0 = g * rows_pg
        hst = st_ref[0, r0:r0 + rows_pg, :]
        bg = b_row[:, g * SSD_STATE:(g + 1) * SSD_STATE]
        cg = c_row[:, g * SSD_STATE:(g + 1) * SSD_STATE]
        hn = dec_col[r0:r0 + rows_pg] * hst + (x_col[r0:r0 + rows_pg] * bg) * dt_col[r0:r0 + rows_pg]
        sto_ref[0, r0:r0 + rows_pg, :] = hn
        y_cols.append(jnp.sum(hn * cg, axis=1, keepdims=True))
    y_col = jnp.concatenate(y_cols, axis=0)
    yt_scr[...] = jnp.where(lane_b, y_col, yt_scr[...])

    @pl.when(b == nb - 1)
    def _():
        y = yt_scr[...].T + dskip_ref[...] * xs_scr[...]
        u = y * _silu(z_ref[...])
        gw = D_INNER // SSD_GROUPS
        for g in range(SSD_GROUPS):
            ug = u[:, g * gw:(g + 1) * gw]
            y_ref[:, g * gw:(g + 1) * gw] = _rmsnorm(ug, gssd_ref[:, g * gw:(g + 1) * gw])


def _ssd_sample(xbc, z, small, sconv, conv_w, conv_b, dtb_pad, a_pad, dskip_full, g_ssd, state):
    nb = xbc.shape[0]
    full = lambda shape: pl.BlockSpec(shape, lambda b: (0,) * len(shape))
    rows = SSD_HEADS * SSD_HEAD_DIM
    st_spec = pl.BlockSpec((1, rows, SSD_STATE), lambda b: (b, 0, 0))
    return pl.pallas_call(
        _ssd_sample_body,
        grid=(nb,),
        in_specs=[full((nb, CONV_DIM)), full((nb, D_INNER)), full((nb, LANES)),
                  full((CONV_W - 1, nb, CONV_DIM)),
                  full((CONV_W, CONV_DIM)), full((1, CONV_DIM)), full((1, LANES)), full((1, LANES)),
                  full((1, D_INNER)), full((1, D_INNER)), st_spec],
        out_specs=[full((nb, D_INNER)), full((CONV_W - 1, nb, CONV_DIM)), st_spec],
        out_shape=[jax.ShapeDtypeStruct((nb, D_INNER), F32),
                   jax.ShapeDtypeStruct((CONV_W - 1, nb, CONV_DIM), F32),
                   jax.ShapeDtypeStruct((nb, rows, SSD_STATE), F32)],
        scratch_shapes=[pltpu.VMEM((nb, D_INNER), F32),
                        pltpu.VMEM((nb, SSD_GROUPS * SSD_STATE), F32),
                        pltpu.VMEM((nb, SSD_GROUPS * SSD_STATE), F32),
                        pltpu.VMEM((D_INNER, nb), F32), pltpu.VMEM((D_INNER, nb), F32),
                        pltpu.VMEM((D_INNER, nb), F32), pltpu.VMEM((D_INNER, nb), F32)],
        compiler_params=pltpu.CompilerParams(dimension_semantics=("arbitrary",), vmem_limit_bytes=VMEM_LIMIT),
        name="ssd_sample",
    )(xbc, z, small, sconv, conv_w, conv_b, dtb_pad, a_pad, dskip_full, g_ssd, state)


class _PagePipeline:
    def __init__(self, pt_ref, pairs, sem, pages, steps_per_seq):
        self.pt_ref, self.pairs, self.sem, self.pages, self.steps_per_seq = pt_ref, pairs, sem, pages, steps_per_seq

    def _copies(self, step, j):
        seq = step // self.steps_per_seq
        chunk = step % self.steps_per_seq
        slot = step % 2
        page = self.pt_ref[seq, chunk * self.pages + j]
        return [pltpu.make_async_copy(src.at[page], buf.at[slot, j], self.sem.at[slot]) for src, buf in self.pairs]

    def start(self, step):
        def body(j, carry):
            for cp in self._copies(step, j):
                cp.start()
            return carry
        lax.fori_loop(0, self.pages, body, 0)

    def wait(self, step):
        slot = step % 2
        for src, buf in self.pairs:
            pltpu.make_async_copy(src.at[pl.ds(0, self.pages)], buf.at[slot], self.sem.at[slot]).wait()

    def advance(self, step, n_steps):
        @pl.when(step == 0)
        def _():
            self.start(step)

        @pl.when(step + 1 < n_steps)
        def _():
            self.start(step + 1)

        self.wait(step)


def _idx_scores_body(pt_ref, qi_ref, wi_ref, cache_hbm, o_ref, buf, sem, *, pages):
    step = pl.program_id(0) * pl.num_programs(1) + pl.program_id(1)
    n_steps = pl.num_programs(0) * pl.num_programs(1)
    _PagePipeline(pt_ref, [(cache_hbm, buf)], sem, pages, pl.num_programs(1)).advance(step, n_steps)
    slot = step % 2
    qi = qi_ref[0]
    wi = wi_ref[0]
    sub = SAMPLE_SUB_PAGES
    for s in range(pages // sub):
        keys_t = jnp.concatenate([buf[slot, s * sub + j] for j in range(sub)], axis=1).astype(BF16)
        d = _dot(qi, keys_t)
        o_ref[0, :, s * sub * PAGE_SIZE:(s + 1) * sub * PAGE_SIZE] = jnp.sum(
            jnp.maximum(d, 0.0) * wi, axis=0, keepdims=True)


def _idx_scores(page_table, qi3, wi3, cache_idx, pages):
    nb, n_pages = page_table.shape
    nch = n_pages // pages
    span = pages * PAGE_SIZE
    grid_spec = pltpu.PrefetchScalarGridSpec(
        num_scalar_prefetch=1,
        grid=(nb, nch),
        in_specs=[pl.BlockSpec((1, IDX_HEADS, IDX_DIM), lambda b, c, pt: (b, 0, 0)),
                  pl.BlockSpec((1, IDX_HEADS, 1), lambda b, c, pt: (b, 0, 0)),
                  pl.BlockSpec(memory_space=pl.ANY)],
        out_specs=pl.BlockSpec((1, 1, span), lambda b, c, pt: (b, 0, c)),
        scratch_shapes=[pltpu.VMEM((2, pages, IDX_DIM, PAGE_SIZE), F32), pltpu.SemaphoreType.DMA((2,))],
    )
    return pl.pallas_call(
        functools.partial(_idx_scores_body, pages=pages),
        grid_spec=grid_spec,
        out_shape=jax.ShapeDtypeStruct((nb, 1, n_pages * PAGE_SIZE), F32),
        compiler_params=pltpu.CompilerParams(dimension_semantics=("arbitrary", "arbitrary"),
                                             vmem_limit_bytes=VMEM_LIMIT),
        name="idx_scores_sample",
    )(page_table, qi3, wi3, cache_idx)


def _select_sample_body(sct_ref, qi_ref, ki_ref, sm_ref, bias_ref, biasn_ref, sc_scr, *, topk):
    length, rows = sct_ref.shape
    sbs = KEY_BLOCK
    for sb in range(length // sbs):
        sc_scr[sb * sbs:(sb + 1) * sbs, :] = _positive_zero(sct_ref[sb * sbs:(sb + 1) * sbs, :])
    ki = ki_ref[...].astype(F32)
    sm = sm_ref[...]
    sc_new = jnp.zeros((rows, 1), F32)
    for h in range(IDX_HEADS):
        d = jnp.sum(qi_ref[:, h * IDX_DIM:(h + 1) * IDX_DIM].astype(F32) * ki, axis=1, keepdims=True)
        sc_new = sc_new + jnp.maximum(d, 0.0) * sm[:, SM_WI + h:SM_WI + h + 1]
    new_row = jnp.broadcast_to(_positive_zero(sc_new), (rows, rows)).T[0:1, :]
    _select_topk_lanes(sc_scr, bias_ref, length // sbs, sbs, topk, None, extra=new_row, extra_bias_ref=biasn_ref)


def _select_sample(scores_t, qib, kib, small, topk):
    length, nb = scores_t.shape
    return pl.pallas_call(
        functools.partial(_select_sample_body, topk=topk),
        out_shape=[jax.ShapeDtypeStruct((length, nb), F32), jax.ShapeDtypeStruct((1, nb), F32)],
        scratch_shapes=[pltpu.VMEM((length, nb), F32)],
        compiler_params=pltpu.CompilerParams(vmem_limit_bytes=VMEM_LIMIT),
        name="select_sample",
    )(scores_t, qib, kib, small)


def _attn_sample_body(pt_ref, q_ref, bias_ref, kn_ref, vn_ref, biasn_ref, ck_hbm, cv_hbm, o_ref,
                      kbuf, vbuf, sem, m_scr, l_scr, acc_scr, *, pages):
    c = pl.program_id(1)
    step = pl.program_id(0) * pl.num_programs(1) + c
    n_steps = pl.num_programs(0) * pl.num_programs(1)
    _PagePipeline(pt_ref, [(ck_hbm, kbuf), (cv_hbm, vbuf)], sem, pages, pl.num_programs(1)).advance(step, n_steps)
    slot = step % 2
    rep = N_HEADS // N_KV_HEADS

    @pl.when(c == 0)
    def _():
        m_scr[...] = jnp.full(m_scr.shape, NEG_BIG, F32)
        l_scr[...] = jnp.zeros_like(l_scr)
        acc_scr[...] = jnp.zeros_like(acc_scr)

    q = q_ref[0]
    head = lax.broadcasted_iota(I32, (N_HEADS, KV_WIDTH), 0)
    lane = lax.broadcasted_iota(I32, (N_HEADS, KV_WIDTH), 1)
    own = (lane // HEAD_DIM) == (head // rep)
    q_bd = jnp.where(own, jnp.concatenate([q.astype(F32)] * N_KV_HEADS, axis=1), 0.0).astype(BF16)
    sub = SAMPLE_SUB_PAGES
    for s in range(pages // sub):
        kk_t = jnp.concatenate([kbuf[slot, s * sub + j] for j in range(sub)], axis=1).astype(BF16)
        vv_t = jnp.concatenate([vbuf[slot, s * sub + j] for j in range(sub)], axis=1).astype(BF16)
        lg = _dot(q_bd, kk_t) + bias_ref[0, :, s * sub * PAGE_SIZE:(s + 1) * sub * PAGE_SIZE]
        m_old = m_scr[...]
        m_new = jnp.maximum(m_old, jnp.max(lg, axis=1, keepdims=True))
        alpha = jnp.exp2(m_old - m_new)
        p = jnp.exp2(lg - m_new)
        l_scr[...] = alpha * l_scr[...] + jnp.sum(p, axis=1, keepdims=True)
        acc_scr[...] = alpha * acc_scr[...] + _dot_nt(p.astype(BF16), vv_t)
        m_scr[...] = m_new

    @pl.when(c == pl.num_programs(1) - 1)
    def _():
        kn = kn_ref[0].astype(F32)
        lg_n = jnp.sum(q_bd.astype(F32) * kn, axis=1, keepdims=True) + biasn_ref[0][:, 0:1]
        m_o = m_scr[...]
        m_n = jnp.maximum(m_o, lg_n)
        al = jnp.exp2(m_o - m_n)
        pn = jnp.exp2(lg_n - m_n)
        l_fin = al * l_scr[...] + pn
        acc = al * acc_scr[...] + pn.astype(BF16).astype(F32) * vn_ref[0].astype(F32)
        res = jnp.where(own, acc / l_fin, 0.0)
        out = res[:, 0:HEAD_DIM]
        for g in range(1, N_KV_HEADS):
            out = out + res[:, g * HEAD_DIM:(g + 1) * HEAD_DIM]
        o_ref[0] = out


def _attn_sample(page_table, q3, bias3, kn3, vn3, biasn3, cache_k, cache_v, pages):
    nb, n_pages = page_table.shape
    nch = n_pages // pages
    span = pages * PAGE_SIZE
    per_b = lambda s: pl.BlockSpec((1,) + s, lambda b, c, pt: (b, 0, 0))
    grid_spec = pltpu.PrefetchScalarGridSpec(
        num_scalar_prefetch=1,
        grid=(nb, nch),
        in_specs=[per_b((N_HEADS, HEAD_DIM)),
                  pl.BlockSpec((1, 1, span), lambda b, c, pt: (b, 0, c)),
                  per_b((1, KV_WIDTH)), per_b((1, KV_WIDTH)), per_b((1, LANES)),
                  pl.BlockSpec(memory_space=pl.ANY), pl.BlockSpec(memory_space=pl.ANY)],
        out_specs=per_b((N_HEADS, HEAD_DIM)),
        scratch_shapes=[pltpu.VMEM((2, pages, KV_WIDTH, PAGE_SIZE), F32),
                        pltpu.VMEM((2, pages, KV_WIDTH, PAGE_SIZE), F32),
                        pltpu.SemaphoreType.DMA((2,)),
                        pltpu.VMEM((N_HEADS, 1), F32), pltpu.VMEM((N_HEADS, 1), F32),
                        pltpu.VMEM((N_HEADS, KV_WIDTH), F32)],
    )
    return pl.pallas_call(
        functools.partial(_attn_sample_body, pages=pages),
        grid_spec=grid_spec,
        out_shape=jax.ShapeDtypeStruct((nb, N_HEADS, HEAD_DIM), F32),
        compiler_params=pltpu.CompilerParams(dimension_semantics=("arbitrary", "arbitrary"),
                                             vmem_limit_bytes=VMEM_LIMIT),
        name="attn_sample",
    )(page_table, q3, bias3, kn3, vn3, biasn3, cache_k, cache_v)


def _rope_tables(pos):
    half = HEAD_DIM // 2
    inv = ROPE_THETA ** (-(jnp.arange(half, dtype=F32) * 2.0) / HEAD_DIM)
    ang = pos.astype(F32)[:, None] * inv[None, :]
    cos = jnp.cos(ang)
    sin = jnp.sin(ang)
    cos_t = jnp.concatenate([cos, cos, cos, cos], axis=1)
    sin_t = jnp.concatenate([-sin, sin, -sin, sin], axis=1)
    return cos_t, sin_t


def _permute_w_in(w):
    sizes = (D_INNER, CONV_DIM, SSD_HEADS, ATTN_WIDTH, KV_WIDTH, KV_WIDTH,
             IDX_HEADS * IDX_DIM, IDX_DIM, IDX_HEADS, D_MODEL, D_MODEL)
    cuts = np.concatenate([[0], np.cumsum(sizes)])
    z, xbc, dt, q, k, v, qi, ki, wi, ga, gb = [w[:, int(cuts[j]):int(cuts[j + 1])] for j in range(len(sizes))]
    pad = jnp.zeros((w.shape[0], SM_KI - SM_WI - IDX_HEADS), w.dtype)
    return jnp.concatenate([z, xbc, q, k, v, qi, ga, gb, dt, wi, pad, ki], axis=1).astype(BF16)


def _pad_lanes(v, offset=0):
    out = jnp.zeros((1, LANES), F32)
    return out.at[0, offset:offset + v.shape[0]].set(v.astype(F32))


def kernel(x_prompt, x_sample, cache_k, cache_v, cache_idx_k, state_ssm, state_conv, page_table, g_mix_pre, w_in,
           g_idx_k, b_idx_k, conv_w, conv_b, dt_bias, a_log, d_skip, g_ssd, w_ssd_out, w_attn_out, w_o, g_mix_post,
           g_ffn_pre, w_gate_up, w_down, g_ffn_post):
    bp, sp, _ = x_prompt.shape
    bd, ts, _ = x_sample.shape
    assert ts == 1 and w_in.shape[0] == 1, "one decode token per sample sequence, depth 1"
    n_pages = page_table.shape[1]
    past = n_pages * PAGE_SIZE
    layer = 0

    w_perm = _permute_w_in(w_in[layer])
    g_pre = g_mix_pre[layer][None, :]
    gik = _pad_lanes(g_idx_k[layer], SM_KI)
    bik = _pad_lanes(b_idx_k[layer], SM_KI)
    dtb = _pad_lanes(dt_bias[layer])
    a_pad = _pad_lanes(a_log[layer])
    dskip_full = jnp.repeat(d_skip[layer].astype(F32), SSD_HEAD_DIM)[None, :]
    gssd = g_ssd[layer][None, :]
    cw = conv_w[layer]
    cbias = conv_b[layer][None, :]
    wso = w_ssd_out[layer].astype(BF16)
    wao = w_attn_out[layer].astype(BF16)
    wo = w_o[layer].astype(BF16)
    wgu = w_gate_up[layer].astype(BF16)
    wd = w_down[layer].astype(BF16)
    gmp = g_mix_post[layer][None, :]
    gfp = g_ffn_pre[layer][None, :]
    gfo = g_ffn_post[layer][None, :]

    tm = 256
    xp2 = x_prompt.reshape(bp * sp, D_MODEL)
    cos_p, sin_p = _rope_tables(jnp.arange(sp))
    tiles_per_seq = sp // tm
    (z, xbc, k, v, ga, gb, ki, small, qb16, kb16, vb16, qib, kib) = _in_projection(
        xp2, cos_p, sin_p, lambda i: (i % tiles_per_seq, 0), g_pre, w_perm, gik, bik, tm)
    y_ssd, ssm_p, tail = _ssd_prompt(xbc, z, small, cw, cbias, dtb, a_pad, dskip_full, gssd, bp, sp)
    wi_t = small[:, SM_WI:SM_WI + IDX_HEADS].T
    y_attn = _dsa_prompt(qib, wi_t, qb16, kib, kb16, _values_feature_major(vb16, PROMPT_KEY_BLOCK), bp, sp)
    yp = _merge_ffn(xp2, y_ssd, y_attn, ga, gb, wso, wao, wo, gmp, gfp, wgu, wd, gfo, tm)

    y_prompt = yp.reshape(bp, sp, D_MODEL)
    k_prompt = k.reshape(1, bp, sp, N_KV_HEADS, HEAD_DIM)
    v_prompt = v.reshape(1, bp, sp, N_KV_HEADS, HEAD_DIM)
    idx_k_prompt = ki.reshape(1, bp, sp, IDX_DIM)
    ssm_prompt = ssm_p[None]
    conv_prompt = tail[None, :, SUBLANES - (CONV_W - 1):, :]

    xs2 = x_sample.reshape(bd, D_MODEL)
    cos_s, sin_s = _rope_tables(jnp.full((bd,), past, jnp.int32))
    (z, xbc, k, v, ga, gb, ki, small, qb16, kb16, vb16, qib, kib) = _in_projection(
        xs2, cos_s, sin_s, lambda i: (i, 0), g_pre, w_perm, gik, bik, bd)
    sconv = jnp.transpose(state_conv[layer], (1, 0, 2))
    st_in = state_ssm[layer].reshape(bd, SSD_HEADS * SSD_HEAD_DIM, SSD_STATE)
    y_ssd, conv_s, st_out = _ssd_sample(xbc, z, small, sconv, cw, cbias, dtb, a_pad, dskip_full, gssd, st_in)

    assert n_pages % SAMPLE_SUB_PAGES == 0, "page count must be a multiple of the per-matmul page group"
    idx_pages = math.gcd(IDX_PAGES_PER_STEP, n_pages)
    attn_pages = math.gcd(ATTN_PAGES_PER_STEP, n_pages)
    topk = min(TOPK_MAX, (past + ts) // 4)
    wi3 = small[:, SM_WI:SM_WI + IDX_HEADS].reshape(bd, IDX_HEADS, 1)
    cidx = jnp.transpose(cache_idx_k[layer], (0, 2, 1))
    ck = jnp.transpose(cache_k[layer], (0, 2, 3, 1)).reshape(-1, KV_WIDTH, PAGE_SIZE)
    cv = jnp.transpose(cache_v[layer], (0, 2, 3, 1)).reshape(-1, KV_WIDTH, PAGE_SIZE)
    scores = _idx_scores(page_table, qib.reshape(bd, IDX_HEADS, IDX_DIM), wi3, cidx, idx_pages)
    bias_t, bias_new = _select_sample(scores.reshape(bd, past).T, qib, kib, small, topk)
    bias_new3 = jnp.broadcast_to(bias_new.reshape(bd, 1, 1), (bd, 1, LANES))
    y_attn = _attn_sample(page_table, qb16.reshape(bd, N_HEADS, HEAD_DIM), bias_t.T.reshape(bd, 1, past),
                          kb16.reshape(bd, 1, KV_WIDTH), vb16.reshape(bd, 1, KV_WIDTH),
                          bias_new3, ck, cv, attn_pages)
    ys = _merge_ffn(xs2, y_ssd, y_attn.reshape(bd, ATTN_WIDTH), ga, gb, wso, wao, wo, gmp, gfp, wgu, wd, gfo, bd)

    y_sample = ys.reshape(bd, ts, D_MODEL)
    k_sample = k.reshape(1, bd, ts, N_KV_HEADS, HEAD_DIM)
    v_sample = v.reshape(1, bd, ts, N_KV_HEADS, HEAD_DIM)
    idx_k_sample = ki.reshape(1, bd, ts, IDX_DIM)
    ssm_sample = st_out.reshape(1, bd, SSD_HEADS, SSD_HEAD_DIM, SSD_STATE)
    conv_sample = jnp.transpose(conv_s, (1, 0, 2))[None]
    return (y_prompt, y_sample, k_prompt, v_prompt, idx_k_prompt, ssm_prompt, conv_prompt,
            k_sample, v_sample, idx_k_sample, ssm_sample, conv_sample)
```

```python
import functools
import math

import numpy as np
import jax
import jax.numpy as jnp
from jax import lax
from jax.experimental import pallas as pl
from jax.experimental.pallas import tpu as pltpu

F32 = jnp.float32
BF16 = jnp.bfloat16
I32 = jnp.int32

D_MODEL = 1024
D_INNER = 2048
SSD_HEAD_DIM = 64
SSD_HEADS = 32
SSD_GROUPS = 4
SSD_STATE = 128
CONV_W = 4
CONV_DIM = D_INNER + 2 * SSD_GROUPS * SSD_STATE
SSD_CHUNK = 128
N_HEADS = 16
N_KV_HEADS = 4
HEAD_DIM = 64
ATTN_WIDTH = N_HEADS * HEAD_DIM
KV_WIDTH = N_KV_HEADS * HEAD_DIM
IDX_HEADS = 8
IDX_DIM = 64
IDX_SCALE = (IDX_HEADS ** -0.5) * (IDX_DIM ** -0.5)
TOPK_MAX = 256
ROPE_THETA = 10000.0
PAGE_SIZE = 128
D_FF = 2816
EPS = 1e-6

LANES = 128
SUBLANES = 8
VMEM_LIMIT = 60 * 1024 * 1024

Z0 = 0
XBC0 = Z0 + D_INNER
Q0 = XBC0 + CONV_DIM
K0 = Q0 + ATTN_WIDTH
V0 = K0 + KV_WIDTH
QI0 = V0 + KV_WIDTH
GA0 = QI0 + IDX_HEADS * IDX_DIM
GB0 = GA0 + D_MODEL
SM0 = GB0 + D_MODEL
N_PROJ = SM0 + LANES
SM_DT = 0
SM_WI = SSD_HEADS
SM_KI = 64

Q_SCALE_LOG2 = (HEAD_DIM ** -0.5) * math.log2(math.e)
NEG_BIG = -1e30
INT_MIN = -(2 ** 31)
F32_MIN_NORMAL_BITS = 0x00800000
TOPK_TRIM_ROUNDS = 2
KEY_BLOCK = 512
PROMPT_Q_BLOCK = 2 * LANES
PROMPT_SCORE_BLOCK = 512
PROMPT_KEY_BLOCK = 256
V_ROWS = 80
IDX_PAGES_PER_STEP = 64
ATTN_PAGES_PER_STEP = 32
SAMPLE_SUB_PAGES = 16


def _dot(a, b):
    return jnp.dot(a, b, preferred_element_type=F32)


def _dot_nt(a, b):
    return lax.dot_general(a, b, (((1,), (1,)), ((), ())), preferred_element_type=F32)


def _sigmoid(x):
    return 1.0 / (1.0 + jnp.exp(-x))


def _silu(x):
    return x * _sigmoid(x)


def _softplus(x):
    return jnp.maximum(x, 0.0) + jnp.log1p(jnp.exp(-jnp.abs(x)))


def _rmsnorm(x, g):
    return x * lax.rsqrt(jnp.mean(x * x, axis=-1, keepdims=True) + EPS) * g


def _rope_tile(x, cos, sin_signed, first_half):
    partner = jnp.where(first_half, pltpu.roll(x, LANES - 32, 1), pltpu.roll(x, 32, 1))
    return x * cos + partner * sin_signed


def _rope_wide(x, cos, sin_signed, first_half):
    n = x.shape[1] // LANES
    return jnp.concatenate(
        [_rope_tile(x[:, c * LANES:(c + 1) * LANES], cos, sin_signed, first_half) for c in range(n)], axis=1)


def _positive_zero(x):
    return jnp.where(x == 0.0, 0.0, x)


def _const_spec(shape):
    nd = len(shape)
    return pl.BlockSpec(shape, lambda *_: (0,) * nd, pipeline_mode=pl.Buffered(1))


def _inproj_body(x_ref, cos_ref, sin_ref, g_ref, w_ref, gik_ref, bik_ref,
                 z_ref, xbc_ref, k_ref, v_ref, ga_ref, gb_ref, ki_ref, sm_ref,
                 qb_ref, kb_ref, vb_ref, qib_ref, kib_ref):
    h = _rmsnorm(x_ref[...], g_ref[...]).astype(BF16)
    cos = cos_ref[...]
    sin = sin_ref[...]
    lane = lax.broadcasted_iota(I32, cos.shape, 1)
    first_half = (lane % HEAD_DIM) < (HEAD_DIM // 2)

    def seg(off, n):
        return _dot(h, w_ref[:, off:off + n])

    z_ref[...] = seg(Z0, D_INNER)
    xbc_ref[...] = seg(XBC0, CONV_DIM)
    ga_ref[...] = seg(GA0, D_MODEL)
    gb_ref[...] = seg(GB0, D_MODEL)
    q = _rope_wide(seg(Q0, ATTN_WIDTH), cos, sin, first_half)
    qb_ref[...] = (q * Q_SCALE_LOG2).astype(BF16)
    k = _rope_wide(seg(K0, KV_WIDTH), cos, sin, first_half)
    k_ref[...] = k
    kb_ref[...] = k.astype(BF16)
    v = seg(V0, KV_WIDTH)
    v_ref[...] = v
    vb_ref[...] = v.astype(BF16)
    qib_ref[...] = _rope_wide(seg(QI0, IDX_HEADS * IDX_DIM), cos, sin, first_half).astype(BF16)
    small = seg(SM0, LANES)
    is_ki = lane >= SM_KI
    mu = jnp.sum(jnp.where(is_ki, small, 0.0), axis=-1, keepdims=True) * (1.0 / IDX_DIM)
    dev = jnp.where(is_ki, small - mu, 0.0)
    var = jnp.sum(dev * dev, axis=-1, keepdims=True) * (1.0 / IDX_DIM)
    kin = dev * lax.rsqrt(var + EPS) * gik_ref[...] + bik_ref[...]
    kir = _rope_tile(kin, cos, sin, first_half)
    ki_ref[...] = kir[:, SM_KI:]
    kib_ref[...] = kir[:, SM_KI:].astype(BF16)
    is_wi = (lane >= SM_WI) & (lane < SM_WI + IDX_HEADS)
    sm_ref[...] = jnp.where(is_wi, small * IDX_SCALE, small)


def _in_projection(x2d, cos_tab, sin_tab, tab_index, g, w_perm, gik, bik, tm):
    m = x2d.shape[0]
    row = lambda n: pl.BlockSpec((tm, n), lambda i: (i, 0))
    out_shapes = [
        jax.ShapeDtypeStruct((m, D_INNER), F32),
        jax.ShapeDtypeStruct((m, CONV_DIM), F32),
        jax.ShapeDtypeStruct((m, KV_WIDTH), F32),
        jax.ShapeDtypeStruct((m, KV_WIDTH), F32),
        jax.ShapeDtypeStruct((m, D_MODEL), F32),
        jax.ShapeDtypeStruct((m, D_MODEL), F32),
        jax.ShapeDtypeStruct((m, IDX_DIM), F32),
        jax.ShapeDtypeStruct((m, LANES), F32),
        jax.ShapeDtypeStruct((m, ATTN_WIDTH), BF16),
        jax.ShapeDtypeStruct((m, KV_WIDTH), BF16),
        jax.ShapeDtypeStruct((m, KV_WIDTH), BF16),
        jax.ShapeDtypeStruct((m, IDX_HEADS * IDX_DIM), BF16),
        jax.ShapeDtypeStruct((m, IDX_DIM), BF16),
    ]
    return pl.pallas_call(
        _inproj_body,
        grid=(m // tm,),
        in_specs=[
            row(D_MODEL),
            pl.BlockSpec((tm, LANES), tab_index),
            pl.BlockSpec((tm, LANES), tab_index),
            _const_spec((1, D_MODEL)),
            _const_spec((D_MODEL, N_PROJ)),
            _const_spec((1, LANES)),
            _const_spec((1, LANES)),
        ],
        out_specs=[row(s.shape[1]) for s in out_shapes],
        out_shape=out_shapes,
        compiler_params=pltpu.CompilerParams(dimension_semantics=("arbitrary",), vmem_limit_bytes=VMEM_LIMIT),
        name="in_projection",
    )(x2d, cos_tab, sin_tab, g, w_perm, gik, bik)


def _pair_cols(col_a, col_b, first):
    return jnp.where(first, col_a, col_b)


def _ssd_prompt_body(xbc_ref, z_ref, sm_ref, cw_ref, cbias_ref, dtb_ref, a_ref, dskip_ref, gssd_ref,
                     y_ref, ssm_ref, tail_ref, xp_scr, st_scr, y_scr):
    c = pl.program_id(1)
    q = SSD_CHUNK

    @pl.when(c == 0)
    def _():
        xp_scr[0:SUBLANES, :] = jnp.zeros((SUBLANES, CONV_DIM), F32)
        st_scr[...] = jnp.zeros_like(st_scr)

    xb = xbc_ref[...]
    xp_scr[SUBLANES:SUBLANES + q, :] = xb
    conv = cbias_ref[...]
    for j in range(CONV_W - 1):
        lo = SUBLANES - (CONV_W - 1) + j
        conv = conv + cw_ref[j:j + 1, :] * xp_scr[lo:lo + q, :]
    conv = conv + cw_ref[CONV_W - 1:CONV_W, :] * xb
    xp_scr[SUBLANES - (CONV_W - 1):SUBLANES, :] = xb[q - (CONV_W - 1):q, :]
    tail_ref[0] = xb[q - SUBLANES:q, :]

    act = _silu(conv)
    xs = act[:, :D_INNER]
    gs = SSD_GROUPS * SSD_STATE
    bs = act[:, D_INNER:D_INNER + gs]
    cs = act[:, D_INNER + gs:]

    lane = lax.broadcasted_iota(I32, (q, LANES), 1)
    rowi = lax.broadcasted_iota(I32, (q, LANES), 0)
    causal = rowi >= lane
    first = lane < SSD_HEAD_DIM
    dt = jnp.where(lane < SSD_HEADS, _softplus(sm_ref[...] + dtb_ref[...]), 0.0)
    dta = dt * -jnp.exp(a_ref[...])
    tril = jnp.where(causal, 1.0, 0.0).astype(F32)
    cum = jnp.dot(tril, dta, preferred_element_type=F32, precision=lax.Precision.HIGHEST)
    cum_t = cum.T
    cum_last = cum[q - 1:q, :]
    e_cum = jnp.exp(cum)
    d_last = jnp.exp(cum_last - cum)
    e_last = jnp.exp(cum_last)

    rpg = SSD_HEADS // SSD_GROUPS
    for g in range(SSD_GROUPS):
        cs_g = cs[:, g * SSD_STATE:(g + 1) * SSD_STATE].astype(BF16)
        bs_g = bs[:, g * SSD_STATE:(g + 1) * SSD_STATE]
        cb = _dot_nt(cs_g, bs_g.astype(BF16))
        bs_t = bs_g.T.astype(BF16)
        for pr in range(rpg // 2):
            h0 = g * rpg + 2 * pr
            h1 = h0 + 1
            hp = h0 // 2
            xs_p = xs[:, hp * LANES:(hp + 1) * LANES]
            xdt = xs_p * _pair_cols(dt[:, h0:h0 + 1], dt[:, h1:h1 + 1], first)
            xdt_b = xdt.astype(BF16)
            yd = []
            for h in (h0, h1):
                seg = jnp.where(causal, cum[:, h:h + 1] - cum_t[h:h + 1, :], -jnp.inf)
                m = (cb * jnp.exp(seg)).astype(BF16)
                yd.append(_dot(m, xdt_b))
            st = st_scr[hp]
            y_off = _dot(cs_g, st.astype(BF16)) * _pair_cols(e_cum[:, h0:h0 + 1], e_cum[:, h1:h1 + 1], first)
            y_scr[:, hp * LANES:(hp + 1) * LANES] = jnp.where(first, yd[0], yd[1]) + y_off
            xdl = xdt * _pair_cols(d_last[:, h0:h0 + 1], d_last[:, h1:h1 + 1], first)
            dec = _pair_cols(e_last[:, h0:h0 + 1], e_last[:, h1:h1 + 1], first[0:1, :])
            st_scr[hp] = dec * st + _dot(bs_t, xdl.astype(BF16))

    y = y_scr[...] + dskip_ref[...] * xs
    u = y * _silu(z_ref[...])
    gw = D_INNER // SSD_GROUPS
    for g in range(SSD_GROUPS):
        ug = u[:, g * gw:(g + 1) * gw]
        y_ref[:, g * gw:(g + 1) * gw] = _rmsnorm(ug, gssd_ref[:, g * gw:(g + 1) * gw])

    @pl.when(c == pl.num_programs(1) - 1)
    def _():
        for hp in range(SSD_HEADS // 2):
            st_t = st_scr[hp].T
            ssm_ref[0, 2 * hp] = st_t[:SSD_HEAD_DIM, :]
            ssm_ref[0, 2 * hp + 1] = st_t[SSD_HEAD_DIM:, :]


def _ssd_prompt(xbc, z, small, conv_w, conv_b, dtb_pad, a_pad, dskip_full, g_ssd, bsz, seq):
    nc = seq // SSD_CHUNK
    q = SSD_CHUNK
    tok = lambda n: pl.BlockSpec((q, n), lambda b, c: (b * nc + c, 0))
    return pl.pallas_call(
        _ssd_prompt_body,
        grid=(bsz, nc),
        in_specs=[
            tok(CONV_DIM), tok(D_INNER), tok(LANES),
            _const_spec((CONV_W, CONV_DIM)), _const_spec((1, CONV_DIM)),
            _const_spec((1, LANES)), _const_spec((1, LANES)),
            _const_spec((1, D_INNER)), _const_spec((1, D_INNER)),
        ],
        out_specs=[
            tok(D_INNER),
            pl.BlockSpec((1, SSD_HEADS, SSD_HEAD_DIM, SSD_STATE), lambda b, c: (b, 0, 0, 0)),
            pl.BlockSpec((1, SUBLANES, CONV_DIM), lambda b, c: (b, 0, 0)),
        ],
        out_shape=[
            jax.ShapeDtypeStruct((bsz * seq, D_INNER), F32),
            jax.ShapeDtypeStruct((bsz, SSD_HEADS, SSD_HEAD_DIM, SSD_STATE), F32),
            jax.ShapeDtypeStruct((bsz, SUBLANES, CONV_DIM), F32),
        ],
        scratch_shapes=[
            pltpu.VMEM((SUBLANES + q, CONV_DIM), F32),
            pltpu.VMEM((SSD_HEADS // 2, SSD_STATE, LANES), F32),
            pltpu.VMEM((q, D_INNER), F32),
        ],
        compiler_params=pltpu.CompilerParams(dimension_semantics=("arbitrary", "arbitrary"),
                                             vmem_limit_bytes=VMEM_LIMIT),
        name="ssd_prompt",
    )(xbc, z, small, conv_w, conv_b, dtb_pad, a_pad, dskip_full, g_ssd)


def _float_of_key(t):
    bits = t ^ ((t >> 31) & 0x7FFFFFFF)
    bits = jnp.where((t >= 1) & (t < F32_MIN_NORMAL_BITS), F32_MIN_NORMAL_BITS, bits)
    return pltpu.bitcast(bits, F32)


def _select_topk_lanes(sc_scr, bias_scr, nsb, sbs, k, valid_at, extra=None, extra_bias_ref=None):
    n_total, qb = sc_scr.shape
    acc_rows = 4 * SUBLANES
    k_off = lax.broadcasted_iota(I32, (sbs, qb), 0)
    extra_idx = jnp.full((1, qb), n_total, I32)

    def fold(fn, init):
        return lax.fori_loop(0, nsb, lambda sb, c: fn(pl.multiple_of(sb * sbs, sbs), c), init)

    def count(pred):
        def body(start, acc):
            hit = pred(sc_scr[pl.ds(start, sbs), :], start + k_off).astype(I32)
            return acc + jnp.sum(hit.reshape(sbs // acc_rows, acc_rows, qb), axis=0)
        tot = jnp.sum(fold(body, jnp.zeros((acc_rows, qb), I32)), axis=0, keepdims=True)
        if extra is not None:
            tot = tot + pred(extra, extra_idx).astype(I32)
        return tot

    def smallest(pred):
        def body(start, acc):
            v = sc_scr[pl.ds(start, sbs), :]
            kept = jnp.where(pred(v, start + k_off), v, jnp.inf)
            return jnp.minimum(acc, jnp.min(kept.reshape(sbs // acc_rows, acc_rows, qb), axis=0))
        out = jnp.min(fold(body, jnp.full((acc_rows, qb), jnp.inf, F32)), axis=0, keepdims=True)
        if extra is not None:
            out = jnp.minimum(out, jnp.where(pred(extra, extra_idx), extra, jnp.inf))
        return out

    def write(select):
        def body(start, carry):
            idx = start + k_off
            sel = select(sc_scr[pl.ds(start, sbs), :], idx)
            if valid_at is not None:
                sel = sel & valid_at(idx)
            bias_scr[pl.ds(start, sbs), :] = jnp.where(sel, 0.0, NEG_BIG).astype(F32)
            return carry
        fold(body, 0)
        if extra is not None:
            extra_bias_ref[...] = jnp.where(select(extra, extra_idx), 0.0, NEG_BIG).astype(F32)

    few = count(lambda v, i: v > -jnp.inf) <= k
    def count_ge(t):
        cand = _float_of_key(t)
        return count(lambda v, i: v >= cand)

    zero = jnp.zeros((1, qb), I32)
    t0 = jnp.where(count_ge(zero) >= k, zero, jnp.full((1, qb), INT_MIN, I32))

    def bit_step(j, t):
        cand = t | jnp.left_shift(jnp.int32(1), 30 - j)
        return jnp.where(count_ge(cand) >= k, cand, t)

    thr = jnp.where(few, -jnp.inf, _float_of_key(lax.fori_loop(0, 31, bit_step, t0)))
    cnt = count(lambda v, i: v >= thr)
    over = (cnt > k) & jnp.logical_not(few)
    any_over = jnp.max(over.astype(I32)) > 0

    @pl.when(jnp.logical_not(any_over))
    def _():
        write(lambda v, i: v >= thr)

    @pl.when(any_over)
    def _():
        lo, strict, n_kept = thr, jnp.zeros((1, qb), jnp.bool_), cnt
        kept = lambda lo, strict: (lambda v, i: (v > lo) | ((v == lo) & jnp.logical_not(strict)))
        for _ in range(TOPK_TRIM_ROUNDS):
            m = smallest(kept(lo, strict))
            n_m = count(lambda v, i: v == m)
            drop = over & (n_kept - n_m >= k)
            lo = jnp.where(drop, m, lo)
            strict = strict | drop
            n_kept = jnp.where(drop, n_kept - n_m, n_kept)
        m = smallest(kept(lo, strict))
        tied = over & (n_kept > k)
        need = k - (n_kept - count(lambda v, i: v == m))
        n_bits = n_total.bit_length()

        def idx_step(j, lim):
            cand = lim | jnp.left_shift(jnp.int32(1), n_bits - 1 - j)
            below = count(lambda v, i: (v == m) & (i < cand))
            return jnp.where(below < need, cand, lim)

        last = jnp.where(tied, lax.fori_loop(0, n_bits, idx_step, jnp.zeros((1, qb), I32)), n_total)
        keep = kept(lo, strict)
        write(lambda v, i: keep(v, i) & ((v != m) | (i <= last)))


def _dsa_prompt_body(qi_ref, wit_ref, q_ref, ki_ref, k_ref, vt_ref, o_ref, sc_scr, bias_scr, m_scr, acc_scr,
                     *, topk):
    i = pl.program_id(1)
    qb = PROMPT_Q_BLOCK
    sbs = PROMPT_SCORE_BLOCK
    kbs = PROMPT_KEY_BLOCK
    n_keys = i * qb + qb
    nsb = (n_keys + sbs - 1) // sbs
    nkb = (n_keys + kbs - 1) // kbs
    q_pos = i * qb + lax.broadcasted_iota(I32, (sbs, qb), 1)
    k_off = lax.broadcasted_iota(I32, (sbs, qb), 0)
    qi_stack = jnp.concatenate(
        [qi_ref[:, h * IDX_DIM:(h + 1) * IDX_DIM] for h in range(IDX_HEADS)], axis=0)
    w_rows = wit_ref[...]

    def score_block(sb, carry):
        start = pl.multiple_of(sb * sbs, sbs)
        d = _dot_nt(ki_ref[pl.ds(start, sbs), :], qi_stack)
        acc = jnp.zeros((sbs, qb), F32)
        for h in range(IDX_HEADS):
            acc = acc + jnp.maximum(d[:, h * qb:(h + 1) * qb], 0.0) * w_rows[h:h + 1, :]
        sc = jnp.where(start + k_off <= q_pos, _positive_zero(acc), -jnp.inf)
        sc_scr[pl.ds(start, sbs), :] = sc
        return carry

    lax.fori_loop(0, nsb, score_block, 0)
    _select_topk_lanes(sc_scr, bias_scr, nsb, sbs, topk, lambda idx: idx <= q_pos)

    rep = N_HEADS // N_KV_HEADS
    m_scr[...] = jnp.full(m_scr.shape, NEG_BIG, m_scr.dtype)
    acc_scr[...] = jnp.zeros_like(acc_scr)

    groups = range(N_KV_HEADS)

    def masked_logits(kb):
        start = pl.multiple_of(kb * kbs, kbs)
        bias = jnp.concatenate([bias_scr[pl.ds(start, kbs), :]] * rep, axis=1)
        out = []
        for g in groups:
            qg = jnp.concatenate(
                [q_ref[:, (g * rep + r) * HEAD_DIM:(g * rep + r + 1) * HEAD_DIM] for r in range(rep)], axis=0)
            lg32 = _dot_nt(k_ref[pl.ds(start, kbs), g * HEAD_DIM:(g + 1) * HEAD_DIM], qg) + bias
            out.append(lg32.astype(BF16))
        return out

    def att_block(kb, carry):
        m_old = [m_scr[g] for g in groups]
        acc_old = [acc_scr[g] for g in groups]
        lg = masked_logits(kb)
        m_new = [jnp.maximum(m_old[g], jnp.max(lg[g], axis=0, keepdims=True)) for g in groups]
        p = [jnp.exp2(lg[g] - m_new[g]) for g in groups]
        pv = [_dot(vt_ref[kb, g], p[g]) for g in groups]
        for g in groups:
            alpha = jnp.exp2(m_old[g].astype(F32) - m_new[g].astype(F32))
            acc_scr[g] = alpha * acc_old[g] + pv[g]
            m_scr[g] = m_new[g]
        return carry

    lax.fori_loop(0, nkb, att_block, 0)

    for g in range(N_KV_HEADS):
        acc = acc_scr[g]
        out_t = acc[:HEAD_DIM, :] / acc[HEAD_DIM:HEAD_DIM + 1, :]
        for pr in range(rep // 2):
            pair = jnp.concatenate([out_t[:, (2 * pr) * qb:(2 * pr + 1) * qb],
                                    out_t[:, (2 * pr + 1) * qb:(2 * pr + 2) * qb]], axis=0)
            c0 = (g * rep + 2 * pr) * HEAD_DIM
            o_ref[:, c0:c0 + 2 * HEAD_DIM] = pair.T


def _dsa_prompt(qib, wi_t, qb16, kib, kb16, vt4, bsz, seq):
    topk = min(TOPK_MAX, seq // 4)
    qblk = PROMPT_Q_BLOCK
    kbs = PROMPT_KEY_BLOCK
    nq = seq // qblk
    tok = lambda n: pl.BlockSpec((qblk, n), lambda b, i: (b * nq + i, 0))
    per_seq = lambda n: pl.BlockSpec((seq, n), lambda b, i: (b, 0))
    return pl.pallas_call(
        functools.partial(_dsa_prompt_body, topk=topk),
        grid=(bsz, nq),
        in_specs=[tok(IDX_HEADS * IDX_DIM),
                  pl.BlockSpec((IDX_HEADS, qblk), lambda b, i: (0, b * nq + i)),
                  tok(ATTN_WIDTH),
                  per_seq(IDX_DIM), per_seq(KV_WIDTH),
                  pl.BlockSpec((seq // kbs, N_KV_HEADS, V_ROWS, kbs), lambda b, i: (b, 0, 0, 0))],
        out_specs=tok(ATTN_WIDTH),
        out_shape=jax.ShapeDtypeStruct((bsz * seq, ATTN_WIDTH), F32),
        scratch_shapes=[pltpu.VMEM((seq, qblk), F32), pltpu.VMEM((seq, qblk), F32),
                        pltpu.VMEM((N_KV_HEADS, 1, (N_HEADS // N_KV_HEADS) * qblk), BF16),
                        pltpu.VMEM((N_KV_HEADS, V_ROWS, (N_HEADS // N_KV_HEADS) * qblk), F32)],
        compiler_params=pltpu.CompilerParams(dimension_semantics=("arbitrary", "arbitrary"),
                                             vmem_limit_bytes=VMEM_LIMIT),
        name="dsa_prompt",
    )(qib, wi_t, qb16, kib, kb16, vt4)


def _values_feature_major(vb16, kbs):
    m = vb16.shape[0]
    v3 = vb16.reshape(m, N_KV_HEADS, HEAD_DIM)
    ones = jnp.ones((m, N_KV_HEADS, 1), BF16)
    pad = jnp.zeros((m, N_KV_HEADS, V_ROWS - HEAD_DIM - 1), BF16)
    v_aug = jnp.concatenate([v3, ones, pad], axis=2)
    return jnp.transpose(v_aug.reshape(m // kbs, kbs, N_KV_HEADS, V_ROWS), (0, 2, 3, 1))


def _merge_ffn_body(x_ref, ys_ref, ya_ref, ga_ref, gb_ref, wso_ref, wao_ref, wo_ref, gmp_ref, gfp_ref,
                    wgu_ref, wd_ref, gfo_ref, o_ref):
    mixed = (_sigmoid(ga_ref[...]) * _dot(ys_ref[...].astype(BF16), wso_ref[...])
             + _sigmoid(gb_ref[...]) * _dot(ya_ref[...].astype(BF16), wao_ref[...]))
    x1 = x_ref[...] + _rmsnorm(_dot(mixed.astype(BF16), wo_ref[...]), gmp_ref[...])
    h2 = _rmsnorm(x1, gfp_ref[...]).astype(BF16)
    gate = _dot(h2, wgu_ref[:, :D_FF])
    up = _dot(h2, wgu_ref[:, D_FF:])
    act = (_silu(gate) * up).astype(BF16)
    o_ref[...] = x1 + _rmsnorm(_dot(act, wd_ref[...]), gfo_ref[...])


def _merge_ffn(x2d, y_ssd, y_attn, ga, gb, wso, wao, wo, gmp, gfp, wgu, wd, gfo, tm):
    m = x2d.shape[0]
    row = lambda n: pl.BlockSpec((tm, n), lambda i: (i, 0))
    return pl.pallas_call(
        _merge_ffn_body,
        grid=(m // tm,),
        in_specs=[row(D_MODEL), row(D_INNER), row(ATTN_WIDTH), row(D_MODEL), row(D_MODEL),
                  _const_spec((D_INNER, D_MODEL)), _const_spec((ATTN_WIDTH, D_MODEL)),
                  _const_spec((D_MODEL, D_MODEL)), _const_spec((1, D_MODEL)), _const_spec((1, D_MODEL)),
                  _const_spec((D_MODEL, 2 * D_FF)), _const_spec((D_FF, D_MODEL)), _const_spec((1, D_MODEL))],
        out_specs=row(D_MODEL),
        out_shape=jax.ShapeDtypeStruct((m, D_MODEL), F32),
        compiler_params=pltpu.CompilerParams(dimension_semantics=("arbitrary",), vmem_limit_bytes=VMEM_LIMIT),
        name="merge_ffn",
    )(x2d, y_ssd, y_attn, ga, gb, wso, wao, wo, gmp, gfp, wgu, wd, gfo)


def _ssd_sample_body(xbc_ref, z_ref, sm_ref, sconv_ref, cw_ref, cbias_ref, dtb_ref, a_ref, dskip_ref,
                     gssd_ref, st_ref, y_ref, conv_ref, sto_ref,
                     xs_scr, bs_scr, cs_scr, xt_scr, dtt_scr, dect_scr, yt_scr):
    b = pl.program_id(0)
    nb = pl.num_programs(0)
    hd = SSD_HEAD_DIM
    gs = SSD_GROUPS * SSD_STATE

    @pl.when(b == 0)
    def _():
        xb = xbc_ref[...]
        conv = cbias_ref[...]
        for j in range(CONV_W - 1):
            conv = conv + cw_ref[j:j + 1, :] * sconv_ref[j]
        conv = conv + cw_ref[CONV_W - 1:CONV_W, :] * xb
        for j in range(CONV_W - 2):
            conv_ref[j] = sconv_ref[j + 1]
        conv_ref[CONV_W - 2] = xb
        act = _silu(conv)
        xs = act[:, :D_INNER]
        xs_scr[...] = xs
        bs_scr[...] = act[:, D_INNER:D_INNER + gs]
        cs_scr[...] = act[:, D_INNER + gs:]
        dt = _softplus(sm_ref[...] + dtb_ref[...])
        dec = jnp.exp(dt * -jnp.exp(a_ref[...]))
        dt_full = jnp.concatenate(
            [jnp.broadcast_to(dt[:, h:h + 1], (dt.shape[0], hd)) for h in range(SSD_HEADS)], axis=1)
        dec_full = jnp.concatenate(
            [jnp.broadcast_to(dec[:, h:h + 1], (dt.shape[0], hd)) for h in range(SSD_HEADS)], axis=1)
        xt_scr[...] = xs.T
        dtt_scr[...] = dt_full.T
        dect_scr[...] = dec_full.T
        yt_scr[...] = jnp.zeros_like(yt_scr)

    nbl = xt_scr.shape[1]
    lane_b = lax.broadcasted_iota(I32, (D_INNER, nbl), 1) == b

    def pick(ref):
        return jnp.sum(jnp.where(lane_b, ref[...], 0.0), axis=1, keepdims=True)

    x_col = pick(xt_scr)
    dt_col = pick(dtt_scr)
    dec_col = pick(dect_scr)
    b_row = bs_scr[pl.ds(b, 1), :]
    c_row = cs_scr[pl.ds(b, 1), :]
    rows_pg = (SSD_HEADS // SSD_GROUPS) * hd
    y_cols = []
    for g in range(SSD_GROUPS):
        r0 = g * rows_pg
        hst = st_ref[0, r0:r0 + rows_pg, :]
        bg = b_row[:, g * SSD_STATE:(g + 1) * SSD_STATE]
        cg = c_row[:, g * SSD_STATE:(g + 1) * SSD_STATE]
        hn = dec_col[r0:r0 + rows_pg] * hst + (x_col[r0:r0 + rows_pg] * bg) * dt_col[r0:r0 + rows_pg]
        sto_ref[0, r0:r0 + rows_pg, :] = hn
        y_cols.append(jnp.sum(hn * cg, axis=1, keepdims=True))
    y_col = jnp.concatenate(y_cols, axis=0)
    yt_scr[...] = jnp.where(lane_b, y_col, yt_scr[...])

    @pl.when(b == nb - 1)
    def _():
        y = yt_scr[...].T + dskip_ref[...] * xs_scr[...]
        u = y * _silu(z_ref[...])
        gw = D_INNER // SSD_GROUPS
        for g in range(SSD_GROUPS):
            ug = u[:, g * gw:(g + 1) * gw]
            y_ref[:, g * gw:(g + 1) * gw] = _rmsnorm(ug, gssd_ref[:, g * gw:(g + 1) * gw])


def _ssd_sample(xbc, z, small, sconv, conv_w, conv_b, dtb_pad, a_pad, dskip_full, g_ssd, state):
    nb = xbc.shape[0]
    full = lambda shape: pl.BlockSpec(shape, lambda b: (0,) * len(shape))
    rows = SSD_HEADS * SSD_HEAD_DIM
    st_spec = pl.BlockSpec((1, rows, SSD_STATE), lambda b: (b, 0, 0))
    return pl.pallas_call(
        _ssd_sample_body,
        grid=(nb,),
        in_specs=[full((nb, CONV_DIM)), full((nb, D_INNER)), full((nb, LANES)),
                  full((CONV_W - 1, nb, CONV_DIM)),
                  full((CONV_W, CONV_DIM)), full((1, CONV_DIM)), full((1, LANES)), full((1, LANES)),
                  full((1, D_INNER)), full((1, D_INNER)), st_spec],
        out_specs=[full((nb, D_INNER)), full((CONV_W - 1, nb, CONV_DIM)), st_spec],
        out_shape=[jax.ShapeDtypeStruct((nb, D_INNER), F32),
                   jax.ShapeDtypeStruct((CONV_W - 1, nb, CONV_DIM), F32),
                   jax.ShapeDtypeStruct((nb, rows, SSD_STATE), F32)],
        scratch_shapes=[pltpu.VMEM((nb, D_INNER), F32),
                        pltpu.VMEM((nb, SSD_GROUPS * SSD_STATE), F32),
                        pltpu.VMEM((nb, SSD_GROUPS * SSD_STATE), F32),
                        pltpu.VMEM((D_INNER, nb), F32), pltpu.VMEM((D_INNER, nb), F32),
                        pltpu.VMEM((D_INNER, nb), F32), pltpu.VMEM((D_INNER, nb), F32)],
        compiler_params=pltpu.CompilerParams(dimension_semantics=("arbitrary",), vmem_limit_bytes=VMEM_LIMIT),
        name="ssd_sample",
    )(xbc, z, small, sconv, conv_w, conv_b, dtb_pad, a_pad, dskip_full, g_ssd, state)


class _PagePipeline:
    def __init__(self, pt_ref, pairs, sem, pages, steps_per_seq):
        self.pt_ref, self.pairs, self.sem, self.pages, self.steps_per_seq = pt_ref, pairs, sem, pages, steps_per_seq

    def _copies(self, step, j):
        seq = step // self.steps_per_seq
        chunk = step % self.steps_per_seq
        slot = step % 2
        page = self.pt_ref[seq, chunk * self.pages + j]
        return [pltpu.make_async_copy(src.at[page], buf.at[slot, j], self.sem.at[slot]) for src, buf in self.pairs]

    def start(self, step):
        def body(j, carry):
            for cp in self._copies(step, j):
                cp.start()
            return carry
        lax.fori_loop(0, self.pages, body, 0)

    def wait(self, step):
        slot = step % 2
        for src, buf in self.pairs:
            pltpu.make_async_copy(src.at[pl.ds(0, self.pages)], buf.at[slot], self.sem.at[slot]).wait()

    def advance(self, step, n_steps):
        @pl.when(step == 0)
        def _():
            self.start(step)

        @pl.when(step + 1 < n_steps)
        def _():
            self.start(step + 1)

        self.wait(step)


def _idx_scores_body(pt_ref, qi_ref, wi_ref, cache_hbm, o_ref, buf, sem, *, pages):
    step = pl.program_id(0) * pl.num_programs(1) + pl.program_id(1)
    n_steps = pl.num_programs(0) * pl.num_programs(1)
    _PagePipeline(pt_ref, [(cache_hbm, buf)], sem, pages, pl.num_programs(1)).advance(step, n_steps)
    slot = step % 2
    qi = qi_ref[0]
    wi = wi_ref[0]
    sub = SAMPLE_SUB_PAGES
    for s in range(pages // sub):
        keys_t = jnp.concatenate([buf[slot, s * sub + j] for j in range(sub)], axis=1).astype(BF16)
        d = _dot(qi, keys_t)
        o_ref[0, :, s * sub * PAGE_SIZE:(s + 1) * sub * PAGE_SIZE] = jnp.sum(
            jnp.maximum(d, 0.0) * wi, axis=0, keepdims=True)


def _idx_scores(page_table, qi3, wi3, cache_idx, pages):
    nb, n_pages = page_table.shape
    nch = n_pages // pages
    span = pages * PAGE_SIZE
    grid_spec = pltpu.PrefetchScalarGridSpec(
        num_scalar_prefetch=1,
        grid=(nb, nch),
        in_specs=[pl.BlockSpec((1, IDX_HEADS, IDX_DIM), lambda b, c, pt: (b, 0, 0)),
                  pl.BlockSpec((1, IDX_HEADS, 1), lambda b, c, pt: (b, 0, 0)),
                  pl.BlockSpec(memory_space=pl.ANY)],
        out_specs=pl.BlockSpec((1, 1, span), lambda b, c, pt: (b, 0, c)),
        scratch_shapes=[pltpu.VMEM((2, pages, IDX_DIM, PAGE_SIZE), F32), pltpu.SemaphoreType.DMA((2,))],
    )
    return pl.pallas_call(
        functools.partial(_idx_scores_body, pages=pages),
        grid_spec=grid_spec,
        out_shape=jax.ShapeDtypeStruct((nb, 1, n_pages * PAGE_SIZE), F32),
        compiler_params=pltpu.CompilerParams(dimension_semantics=("arbitrary", "arbitrary"),
                                             vmem_limit_bytes=VMEM_LIMIT),
        name="idx_scores_sample",
    )(page_table, qi3, wi3, cache_idx)


def _select_sample_body(sct_ref, qi_ref, ki_ref, sm_ref, bias_ref, biasn_ref, sc_scr, *, topk):
    length, rows = sct_ref.shape
    sbs = KEY_BLOCK
    for sb in range(length // sbs):
        sc_scr[sb * sbs:(sb + 1) * sbs, :] = _positive_zero(sct_ref[sb * sbs:(sb + 1) * sbs, :])
    ki = ki_ref[...].astype(F32)
    sm = sm_ref[...]
    sc_new = jnp.zeros((rows, 1), F32)
    for h in range(IDX_HEADS):
        d = jnp.sum(qi_ref[:, h * IDX_DIM:(h + 1) * IDX_DIM].astype(F32) * ki, axis=1, keepdims=True)
        sc_new = sc_new + jnp.maximum(d, 0.0) * sm[:, SM_WI + h:SM_WI + h + 1]
    new_row = jnp.broadcast_to(_positive_zero(sc_new), (rows, rows)).T[0:1, :]
    _select_topk_lanes(sc_scr, bias_ref, length // sbs, sbs, topk, None, extra=new_row, extra_bias_ref=biasn_ref)


def _select_sample(scores_t, qib, kib, small, topk):
    length, nb = scores_t.shape
    return pl.pallas_call(
        functools.partial(_select_sample_body, topk=topk),
        out_shape=[jax.ShapeDtypeStruct((length, nb), F32), jax.ShapeDtypeStruct((1, nb), F32)],
        scratch_shapes=[pltpu.VMEM((length, nb), F32)],
        compiler_params=pltpu.CompilerParams(vmem_limit_bytes=VMEM_LIMIT),
        name="select_sample",
    )(scores_t, qib, kib, small)


def _attn_sample_body(pt_ref, q_ref, bias_ref, kn_ref, vn_ref, biasn_ref, ck_hbm, cv_hbm, o_ref,
                      kbuf, vbuf, sem, m_scr, l_scr, acc_scr, *, pages):
    c = pl.program_id(1)
    step = pl.program_id(0) * pl.num_programs(1) + c
    n_steps = pl.num_programs(0) * pl.num_programs(1)
    _PagePipeline(pt_ref, [(ck_hbm, kbuf), (cv_hbm, vbuf)], sem, pages, pl.num_programs(1)).advance(step, n_steps)
    slot = step % 2
    rep = N_HEADS // N_KV_HEADS

    @pl.when(c == 0)
    def _():
        m_scr[...] = jnp.full(m_scr.shape, NEG_BIG, F32)
        l_scr[...] = jnp.zeros_like(l_scr)
        acc_scr[...] = jnp.zeros_like(acc_scr)

    q = q_ref[0]
    head = lax.broadcasted_iota(I32, (N_HEADS, KV_WIDTH), 0)
    lane = lax.broadcasted_iota(I32, (N_HEADS, KV_WIDTH), 1)
    own = (lane // HEAD_DIM) == (head // rep)
    q_bd = jnp.where(own, jnp.concatenate([q.astype(F32)] * N_KV_HEADS, axis=1), 0.0).astype(BF16)
    sub = SAMPLE_SUB_PAGES
    for s in range(pages // sub):
        kk_t = jnp.concatenate([kbuf[slot, s * sub + j] for j in range(sub)], axis=1).astype(BF16)
        vv_t = jnp.concatenate([vbuf[slot, s * sub + j] for j in range(sub)], axis=1).astype(BF16)
        lg = _dot(q_bd, kk_t) + bias_ref[0, :, s * sub * PAGE_SIZE:(s + 1) * sub * PAGE_SIZE]
        m_old = m_scr[...]
        m_new = jnp.maximum(m_old, jnp.max(lg, axis=1, keepdims=True))
        alpha = jnp.exp2(m_old - m_new)
        p = jnp.exp2(lg - m_new)
        l_scr[...] = alpha * l_scr[...] + jnp.sum(p, axis=1, keepdims=True)
        acc_scr[...] = alpha * acc_scr[...] + _dot_nt(p.astype(BF16), vv_t)
        m_scr[...] = m_new

    @pl.when(c == pl.num_programs(1) - 1)
    def _():
        kn = kn_ref[0].astype(F32)
        lg_n = jnp.sum(q_bd.astype(F32) * kn, axis=1, keepdims=True) + biasn_ref[0][:, 0:1]
        m_o = m_scr[...]
        m_n = jnp.maximum(m_o, lg_n)
        al = jnp.exp2(m_o - m_n)
        pn = jnp.exp2(lg_n - m_n)
        l_fin = al * l_scr[...] + pn
        acc = al * acc_scr[...] + pn.astype(BF16).astype(F32) * vn_ref[0].astype(F32)
        res = jnp.where(own, acc / l_fin, 0.0)
        out = res[:, 0:HEAD_DIM]
        for g in range(1, N_KV_HEADS):
            out = out + res[:, g * HEAD_DIM:(g + 1) * HEAD_DIM]
        o_ref[0] = out


def _attn_sample(page_table, q3, bias3, kn3, vn3, biasn3, cache_k, cache_v, pages):
    nb, n_pages = page_table.shape
    nch = n_pages // pages
    span = pages * PAGE_SIZE
    per_b = lambda s: pl.BlockSpec((1,) + s, lambda b, c, pt: (b, 0, 0))
    grid_spec = pltpu.PrefetchScalarGridSpec(
        num_scalar_prefetch=1,
        grid=(nb, nch),
        in_specs=[per_b((N_HEADS, HEAD_DIM)),
                  pl.BlockSpec((1, 1, span), lambda b, c, pt: (b, 0, c)),
                  per_b((1, KV_WIDTH)), per_b((1, KV_WIDTH)), per_b((1, LANES)),
                  pl.BlockSpec(memory_space=pl.ANY), pl.BlockSpec(memory_space=pl.ANY)],
        out_specs=per_b((N_HEADS, HEAD_DIM)),
        scratch_shapes=[pltpu.VMEM((2, pages, KV_WIDTH, PAGE_SIZE), F32),
                        pltpu.VMEM((2, pages, KV_WIDTH, PAGE_SIZE), F32),
                        pltpu.SemaphoreType.DMA((2,)),
                        pltpu.VMEM((N_HEADS, 1), F32), pltpu.VMEM((N_HEADS, 1), F32),
                        pltpu.VMEM((N_HEADS, KV_WIDTH), F32)],
    )
    return pl.pallas_call(
        functools.partial(_attn_sample_body, pages=pages),
        grid_spec=grid_spec,
        out_shape=jax.ShapeDtypeStruct((nb, N_HEADS, HEAD_DIM), F32),
        compiler_params=pltpu.CompilerParams(dimension_semantics=("arbitrary", "arbitrary"),
                                             vmem_limit_bytes=VMEM_LIMIT),
        name="attn_sample",
    )(page_table, q3, bias3, kn3, vn3, biasn3, cache_k, cache_v)


def _rope_tables(pos):
    half = HEAD_DIM // 2
    inv = ROPE_THETA ** (-(jnp.arange(half, dtype=F32) * 2.0) / HEAD_DIM)
    ang = pos.astype(F32)[:, None] * inv[None, :]
    cos = jnp.cos(ang)
    sin = jnp.sin(ang)
    cos_t = jnp.concatenate([cos, cos, cos, cos], axis=1)
    sin_t = jnp.concatenate([-sin, sin, -sin, sin], axis=1)
    return cos_t, sin_t


def _permute_w_in(w):
    sizes = (D_INNER, CONV_DIM, SSD_HEADS, ATTN_WIDTH, KV_WIDTH, KV_WIDTH,
             IDX_HEADS * IDX_DIM, IDX_DIM, IDX_HEADS, D_MODEL, D_MODEL)
    cuts = np.concatenate([[0], np.cumsum(sizes)])
    z, xbc, dt, q, k, v, qi, ki, wi, ga, gb = [w[:, int(cuts[j]):int(cuts[j + 1])] for j in range(len(sizes))]
    pad = jnp.zeros((w.shape[0], SM_KI - SM_WI - IDX_HEADS), w.dtype)
    return jnp.concatenate([z, xbc, q, k, v, qi, ga, gb, dt, wi, pad, ki], axis=1).astype(BF16)


def _pad_lanes(v, offset=0):
    out = jnp.zeros((1, LANES), F32)
    return out.at[0, offset:offset + v.shape[0]].set(v.astype(F32))


def kernel(x_prompt, x_sample, cache_k, cache_v, cache_idx_k, state_ssm, state_conv, page_table, g_mix_pre, w_in,
           g_idx_k, b_idx_k, conv_w, conv_b, dt_bias, a_log, d_skip, g_ssd, w_ssd_out, w_attn_out, w_o, g_mix_post,
           g_ffn_pre, w_gate_up, w_down, g_ffn_post):
    bp, sp, _ = x_prompt.shape
    bd, ts, _ = x_sample.shape
    assert ts == 1 and w_in.shape[0] == 1, "one decode token per sample sequence, depth 1"
    n_pages = page_table.shape[1]
    past = n_pages * PAGE_SIZE
    layer = 0

    w_perm = _permute_w_in(w_in[layer])
    g_pre = g_mix_pre[layer][None, :]
    gik = _pad_lanes(g_idx_k[layer], SM_KI)
    bik = _pad_lanes(b_idx_k[layer], SM_KI)
    dtb = _pad_lanes(dt_bias[layer])
    a_pad = _pad_lanes(a_log[layer])
    dskip_full = jnp.repeat(d_skip[layer].astype(F32), SSD_HEAD_DIM)[None, :]
    gssd = g_ssd[layer][None, :]
    cw = conv_w[layer]
    cbias = conv_b[layer][None, :]
    wso = w_ssd_out[layer].astype(BF16)
    wao = w_attn_out[layer].astype(BF16)
    wo = w_o[layer].astype(BF16)
    wgu = w_gate_up[layer].astype(BF16)
    wd = w_down[layer].astype(BF16)
    gmp = g_mix_post[layer][None, :]
    gfp = g_ffn_pre[layer][None, :]
    gfo = g_ffn_post[layer][None, :]

    tm = 256
    xp2 = x_prompt.reshape(bp * sp, D_MODEL)
    cos_p, sin_p = _rope_tables(jnp.arange(sp))
    tiles_per_seq = sp // tm
    (z, xbc, k, v, ga, gb, ki, small, qb16, kb16, vb16, qib, kib) = _in_projection(
        xp2, cos_p, sin_p, lambda i: (i % tiles_per_seq, 0), g_pre, w_perm, gik, bik, tm)
    y_ssd, ssm_p, tail = _ssd_prompt(xbc, z, small, cw, cbias, dtb, a_pad, dskip_full, gssd, bp, sp)
    wi_t = small[:, SM_WI:SM_WI + IDX_HEADS].T
    y_attn = _dsa_prompt(qib, wi_t, qb16, kib, kb16, _values_feature_major(vb16, PROMPT_KEY_BLOCK), bp, sp)
    yp = _merge_ffn(xp2, y_ssd, y_attn, ga, gb, wso, wao, wo, gmp, gfp, wgu, wd, gfo, tm)

    y_prompt = yp.reshape(bp, sp, D_MODEL)
    k_prompt = k.reshape(1, bp, sp, N_KV_HEADS, HEAD_DIM)
    v_prompt = v.reshape(1, bp, sp, N_KV_HEADS, HEAD_DIM)
    idx_k_prompt = ki.reshape(1, bp, sp, IDX_DIM)
    ssm_prompt = ssm_p[None]
    conv_prompt = tail[None, :, SUBLANES - (CONV_W - 1):, :]

    xs2 = x_sample.reshape(bd, D_MODEL)
    cos_s, sin_s = _rope_tables(jnp.full((bd,), past, jnp.int32))
    (z, xbc, k, v, ga, gb, ki, small, qb16, kb16, vb16, qib, kib) = _in_projection(
        xs2, cos_s, sin_s, lambda i: (i, 0), g_pre, w_perm, gik, bik, bd)
    sconv = jnp.transpose(state_conv[layer], (1, 0, 2))
    st_in = state_ssm[layer].reshape(bd, SSD_HEADS * SSD_HEAD_DIM, SSD_STATE)
    y_ssd, conv_s, st_out = _ssd_sample(xbc, z, small, sconv, cw, cbias, dtb, a_pad, dskip_full, gssd, st_in)

    assert n_pages % SAMPLE_SUB_PAGES == 0, "page count must be a multiple of the per-matmul page group"
    idx_pages = math.gcd(IDX_PAGES_PER_STEP, n_pages)
    attn_pages = math.gcd(ATTN_PAGES_PER_STEP, n_pages)
    topk = min(TOPK_MAX, (past + ts) // 4)
    wi3 = small[:, SM_WI:SM_WI + IDX_HEADS].reshape(bd, IDX_HEADS, 1)
    cidx = jnp.transpose(cache_idx_k[layer], (0, 2, 1))
    ck = jnp.transpose(cache_k[layer], (0, 2, 3, 1)).reshape(-1, KV_WIDTH, PAGE_SIZE)
    cv = jnp.transpose(cache_v[layer], (0, 2, 3, 1)).reshape(-1, KV_WIDTH, PAGE_SIZE)
    scores = _idx_scores(page_table, qib.reshape(bd, IDX_HEADS, IDX_DIM), wi3, cidx, idx_pages)
    bias_t, bias_new = _select_sample(scores.reshape(bd, past).T, qib, kib, small, topk)
    bias_new3 = jnp.broadcast_to(bias_new.reshape(bd, 1, 1), (bd, 1, LANES))
    y_attn = _attn_sample(page_table, qb16.reshape(bd, N_HEADS, HEAD_DIM), bias_t.T.reshape(bd, 1, past),
                          kb16.reshape(bd, 1, KV_WIDTH), vb16.reshape(bd, 1, KV_WIDTH),
                          bias_new3, ck, cv, attn_pages)
    ys = _merge_ffn(xs2, y_ssd, y_attn.reshape(bd, ATTN_WIDTH), ga, gb, wso, wao, wo, gmp, gfp, wgu, wd, gfo, bd)

    y_sample = ys.reshape(bd, ts, D_MODEL)
    k_sample = k.reshape(1, bd, ts, N_KV_HEADS, HEAD_DIM)
    v_sample = v.reshape(1, bd, ts, N_KV_HEADS, HEAD_DIM)
    idx_k_sample = ki.reshape(1, bd, ts, IDX_DIM)
    ssm_sample = st_out.reshape(1, bd, SSD_HEADS, SSD_HEAD_DIM, SSD_STATE)
    conv_sample = jnp.transpose(conv_s, (1, 0, 2))[None]
    return (y_prompt, y_sample, k_prompt, v_prompt, idx_k_prompt, ssm_prompt, conv_prompt,
            k_sample, v_sample, idx_k_sample, ssm_sample, conv_sample)
```
